```python
import jax, jax.numpy as jnp
from jax import lax
import numpy as np

D_MODEL = 1024
BATCH = 8
SEQ = 2048
DEPTH = 4
DEC_BATCH = 128
DEC_SEQ = 1
PAST_LEN = 8192
PAGE_SIZE = 128

HEAD_DIM = 64
A_WIDTH = D_MODEL // 2
CONV_W = 3
B_HEADS = (D_MODEL // 2) // HEAD_DIM
B_KV_HEADS = 2
SWA_WINDOW = 128
C_WIDTH = D_MODEL // 2
POOL_WINDOWS = (2, 4, 8, 16)
POOL_GROUPS = len(POOL_WINDOWS)
POOL_GC = C_WIDTH // POOL_GROUPS
POOL_PAD = max(POOL_WINDOWS) - 1
D_HEADS = (D_MODEL // 2) // HEAD_DIM
DIL_PATTERNS = ((128, 1), (512, 4), (2048, 16))
DIL_MAX_WINDOW = max(w for w, _ in DIL_PATTERNS)
MEM_LEN = 256
CA_HEADS = 4
CA_HEAD_DIM = D_MODEL // CA_HEADS
FFN_HIDDEN = -(-8 * D_MODEL // (3 * 256)) * 256
N_EVEN = (DEPTH + 1) // 2
N_ODD = DEPTH // 2
EVEN_IN = 3 * A_WIDTH + (B_HEADS + 2 * B_KV_HEADS) * HEAD_DIM
EVEN_SPLITS = (A_WIDTH, 2 * A_WIDTH, 3 * A_WIDTH, 3 * A_WIDTH + B_HEADS * HEAD_DIM,
               3 * A_WIDTH + (B_HEADS + B_KV_HEADS) * HEAD_DIM)
EVEN_MIX = A_WIDTH + B_HEADS * HEAD_DIM
ODD_IN = C_WIDTH + 3 * D_HEADS * HEAD_DIM
ODD_SPLITS = (C_WIDTH, C_WIDTH + D_HEADS * HEAD_DIM, C_WIDTH + 2 * D_HEADS * HEAD_DIM)
ODD_MIX = C_WIDTH + D_HEADS * HEAD_DIM
RMS_EPS = 1e-6
NEG_INF = -1e30

kernel_name = "hybrid_conv_swa_pool_dilated_decoder_step"


def rms_norm(x, g):
    xf = x.astype(jnp.float32)
    y = xf * lax.rsqrt(jnp.mean(xf * xf, axis=-1, keepdims=True) + RMS_EPS)
    return (y * g.astype(jnp.float32)).astype(x.dtype)


def attend(q, k, v, mask):
    *lead, tq, h, dh = q.shape
    hk = k.shape[-2]
    qg = q.reshape(*lead, tq, hk, h // hk, dh)
    s = jnp.einsum("...qhgd,...khd->...hgqk", qg, k, preferred_element_type=jnp.float32) * (dh ** -0.5)
    s = jnp.where(mask[..., None, None, :, :], s, NEG_INF)
    lse = jax.nn.logsumexp(s, axis=-1)
    p = jnp.exp(s - lse[..., None]).astype(v.dtype)
    o = jnp.einsum("...hgqk,...khd->...qhgd", p, v).reshape(*lead, tq, h, dh)
    lse = jnp.swapaxes(lse.reshape(*lead, h, tq), -1, -2)
    return o, lse


def band_attn(q, k, v, window):
    bn, L, h, dh = q.shape
    blk = window
    nb = -(-L // blk)
    pad = nb * blk - L
    qb = jnp.pad(q, ((0, 0), (0, pad), (0, 0), (0, 0))).reshape(bn, nb, blk, h, dh)

    def key_blocks(a):
        a = jnp.pad(a, ((0, 0), (blk, pad), (0, 0), (0, 0))).reshape(bn, nb + 1, blk, a.shape[2], dh)
        return jnp.concatenate([a[:, :-1], a[:, 1:]], axis=2)

    kb, vb = key_blocks(k), key_blocks(v)
    ki = jnp.arange(2 * blk)
    delta = (jnp.arange(blk)[:, None] + blk) - ki[None, :]
    kabs = (jnp.arange(nb)[:, None, None] - 1) * blk + ki[None, None, :]
    mask = (delta >= 0)[None] & (delta <= window)[None] & (kabs >= 0)
    o, lse = attend(qb, kb, vb, mask)
    return o.reshape(bn, nb * blk, h, dh)[:, :L], lse.reshape(bn, nb * blk, h)[:, :L]


def apply_sinks(o, lse, sinks):
    gate = jax.nn.sigmoid(lse - sinks.astype(jnp.float32))
    return (o * gate[..., None]).astype(o.dtype)


def combine_branches(outs, lses):
    wts = jax.nn.softmax(jnp.stack(lses), axis=0)
    y = sum(wts[i][..., None] * outs[i] for i in range(len(outs)))
    return y.astype(outs[0].dtype)


def dilated_prompt(q, k, v):
    bn, L, h, dh = q.shape
    outs, lses = [], []
    for w, d in DIL_PATTERNS:
        def sub(a):
            return a.reshape(bn, L // d, d, a.shape[2], dh).transpose(0, 2, 1, 3, 4).reshape(bn * d, L // d, a.shape[2], dh)
        o, lse = band_attn(sub(q), sub(k), sub(v), w // d)
        outs.append(o.reshape(bn, d, L // d, h, dh).transpose(0, 2, 1, 3, 4).reshape(bn, L, h, dh))
        lses.append(lse.reshape(bn, d, L // d, h).transpose(0, 2, 1, 3).reshape(bn, L, h))
    return combine_branches(outs, lses)


def dilated_sample(q, k_buf, v_buf, k, v):
    bn, t, h, dh = q.shape
    nbuf = k_buf.shape[1]
    kc = jnp.concatenate([k_buf, k], axis=1)
    vc = jnp.concatenate([v_buf, v], axis=1)
    qpos = PAST_LEN + jnp.arange(t)
    outs, lses = [], []
    for w, d in DIL_PATTERNS:
        kpos = qpos[:, None] - d * jnp.arange(w // d + 1)[None, :]
        idx = jnp.clip(kpos - (PAST_LEN - nbuf), 0, nbuf + t - 1)
        o, lse = attend(q[:, :, None], kc[:, idx], vc[:, idx], (kpos >= 0)[:, None, :])
        outs.append(o[:, :, 0])
        lses.append(lse[:, :, 0])
    return combine_branches(outs, lses)


def short_conv(upad, w):
    L = upad.shape[1] - (CONV_W - 1)
    return sum(w[i] * upad[:, i:i + L] for i in range(CONV_W))


def pool_mix(upad, u, pos, pool_w, pool_scale):
    bn, L, c = u.shape
    cs = jnp.pad(jnp.cumsum(upad.astype(jnp.float32), axis=1), ((0, 0), (1, 0), (0, 0)))
    top = cs[:, POOL_PAD + 1:]
    means = []
    for g, w in enumerate(POOL_WINDOWS):
        sl = slice(g * POOL_GC, (g + 1) * POOL_GC)
        win_sum = top[:, :, sl] - cs[:, POOL_PAD + 1 - w:POOL_PAD + 1 - w + L, sl]
        cnt = jnp.minimum(pos + 1, w).astype(jnp.float32)[:, None]
        means.append(win_sum / cnt)
    dlt = (jnp.concatenate(means, axis=-1) - u.astype(jnp.float32)).astype(u.dtype)
    y = jnp.einsum("blgc,gcd->blgd", dlt.reshape(bn, L, POOL_GROUPS, POOL_GC), pool_w)
    return y.reshape(bn, L, c) * pool_scale


def even_mixer_prompt(h, w_in, conv_w, sinks, w_out):
    bn, L, _ = h.shape
    bg, cg, xa, q, k, v = jnp.split(h @ w_in, EVEN_SPLITS, axis=-1)
    u = cg * xa
    upad = jnp.concatenate([jnp.zeros((bn, CONV_W - 1, A_WIDTH), u.dtype), u], axis=1)
    ya = bg * short_conv(upad, conv_w)
    q = q.reshape(bn, L, B_HEADS, HEAD_DIM)
    k = k.reshape(bn, L, B_KV_HEADS, HEAD_DIM)
    v = v.reshape(bn, L, B_KV_HEADS, HEAD_DIM)
    o, lse = band_attn(q, k, v, SWA_WINDOW)
    yb = apply_sinks(o, lse, sinks)
    out = jnp.concatenate([ya, yb.reshape(bn, L, -1)], axis=-1) @ w_out
    keep = min(SWA_WINDOW, L)
    return out, u[:, L - (CONV_W - 1):], k[:, L - keep:], v[:, L - keep:]


def even_mixer_sample(h, conv_buf, k_buf, v_buf, w_in, conv_w, sinks, w_out):
    bn, t, _ = h.shape
    bg, cg, xa, q, k, v = jnp.split(h @ w_in, EVEN_SPLITS, axis=-1)
    upad = jnp.concatenate([conv_buf, cg * xa], axis=1)
    ya = bg * short_conv(upad, conv_w)
    q = q.reshape(bn, t, B_HEADS, HEAD_DIM)
    k = k.reshape(bn, t, B_KV_HEADS, HEAD_DIM)
    v = v.reshape(bn, t, B_KV_HEADS, HEAD_DIM)
    nbuf = k_buf.shape[1]
    kc = jnp.concatenate([k_buf, k], axis=1)
    vc = jnp.concatenate([v_buf, v], axis=1)
    qpos = PAST_LEN + jnp.arange(t)
    kpos = PAST_LEN - nbuf + jnp.arange(nbuf + t)
    delta = qpos[:, None] - kpos[None, :]
    o, lse = attend(q, kc, vc, (delta >= 0) & (delta <= SWA_WINDOW))
    yb = apply_sinks(o, lse, sinks)
    out = jnp.concatenate([ya, yb.reshape(bn, t, -1)], axis=-1) @ w_out
    return out, upad[:, -(CONV_W - 1):], k, v


def odd_mixer_prompt(h, w_in, pool_w, pool_scale, w_out):
    bn, L, _ = h.shape
    uc, q, k, v = jnp.split(h @ w_in, ODD_SPLITS, axis=-1)
    upad = jnp.concatenate([jnp.zeros((bn, POOL_PAD, C_WIDTH), uc.dtype), uc], axis=1)
    yc = pool_mix(upad, uc, jnp.arange(L), pool_w, pool_scale)
    q = q.reshape(bn, L, D_HEADS, HEAD_DIM)
    k = k.reshape(bn, L, D_HEADS, HEAD_DIM)
    v = v.reshape(bn, L, D_HEADS, HEAD_DIM)
    yd = dilated_prompt(q, k, v)
    out = jnp.concatenate([yc, yd.reshape(bn, L, -1)], axis=-1) @ w_out
    keep = min(DIL_MAX_WINDOW, L)
    return out, uc[:, L - POOL_PAD:], k[:, L - keep:], v[:, L - keep:]


def odd_mixer_sample(h, pool_buf, k_buf, v_buf, w_in, pool_w, pool_scale, w_out):
    bn, t, _ = h.shape
    uc, q, k, v = jnp.split(h @ w_in, ODD_SPLITS, axis=-1)
    upad = jnp.concatenate([pool_buf, uc], axis=1)
    yc = pool_mix(upad, uc, PAST_LEN + jnp.arange(t), pool_w, pool_scale)
    q = q.reshape(bn, t, D_HEADS, HEAD_DIM)
    k = k.reshape(bn, t, D_HEADS, HEAD_DIM)
    v = v.reshape(bn, t, D_HEADS, HEAD_DIM)
    yd = dilated_sample(q, k_buf, v_buf, k, v)
    out = jnp.concatenate([yc, yd.reshape(bn, t, -1)], axis=-1) @ w_out
    return out, upad[:, -POOL_PAD:], k, v


def memory_kv(mem, g, w_kv):
    bn, m, _ = mem.shape
    mk, mv = jnp.split(rms_norm(mem, g) @ w_kv, 2, axis=-1)
    return mk.reshape(bn, m, CA_HEADS, CA_HEAD_DIM), mv.reshape(bn, m, CA_HEADS, CA_HEAD_DIM)


def cross_attn(h, mk, mv, w_q, w_o):
    bn, L, _ = h.shape
    q = (h @ w_q).reshape(bn, L, CA_HEADS, CA_HEAD_DIM)
    o, _ = attend(q, mk, mv, jnp.ones((L, mk.shape[1]), dtype=bool))
    return o.reshape(bn, L, -1) @ w_o


def swiglu(h, w_in, w_out):
    g, u = jnp.split(h @ w_in, 2, axis=-1)
    return (jax.nn.silu(g) * u) @ w_out


def setup_inputs(seed: int = 0) -> dict:
    key = jax.random.key(seed)
    ks = iter(jax.random.split(key, 40))

    def nrm(shape, scale=1.0):
        return scale * jax.random.normal(next(ks), shape, jnp.float32)

    swa_keep = min(SWA_WINDOW, PAST_LEN)
    dil_keep = min(DIL_MAX_WINDOW, PAST_LEN)
    return {
        "x_prompt": nrm((BATCH, SEQ, D_MODEL)),
        "x_sample": nrm((DEC_BATCH, DEC_SEQ, D_MODEL)),
        "mem_prompt": nrm((BATCH, MEM_LEN, D_MODEL)),
        "cache_conv": nrm((N_EVEN, DEC_BATCH, CONV_W - 1, A_WIDTH)),
        "cache_swa_k": nrm((N_EVEN, DEC_BATCH, swa_keep, B_KV_HEADS, HEAD_DIM)),
        "cache_swa_v": nrm((N_EVEN, DEC_BATCH, swa_keep, B_KV_HEADS, HEAD_DIM)),
        "state_pool": nrm((N_ODD, DEC_BATCH, POOL_PAD, C_WIDTH)),
        "cache_dil_k": nrm((N_ODD, DEC_BATCH, dil_keep, D_HEADS, HEAD_DIM)),
        "cache_dil_v": nrm((N_ODD, DEC_BATCH, dil_keep, D_HEADS, HEAD_DIM)),
        "cache_mem_k": nrm((DEPTH, DEC_BATCH, MEM_LEN, CA_HEADS, CA_HEAD_DIM)),
        "cache_mem_v": nrm((DEPTH, DEC_BATCH, MEM_LEN, CA_HEADS, CA_HEAD_DIM)),
        "norm_mix": 1.0 + nrm((DEPTH, D_MODEL), 0.05),
        "norm_ca": 1.0 + nrm((DEPTH, D_MODEL), 0.05),
        "norm_mem": 1.0 + nrm((DEPTH, D_MODEL), 0.05),
        "norm_ffn": 1.0 + nrm((DEPTH, D_MODEL), 0.05),
        "norm_final": 1.0 + nrm((D_MODEL,), 0.05),
        "w_in_even": nrm((N_EVEN, D_MODEL, EVEN_IN), D_MODEL ** -0.5),
        "conv_w": nrm((N_EVEN, CONV_W, A_WIDTH), CONV_W ** -0.5),
        "sinks": nrm((N_EVEN, B_HEADS), 0.5),
        "w_out_even": nrm((N_EVEN, EVEN_MIX, D_MODEL), EVEN_MIX ** -0.5),
        "w_in_odd": nrm((N_ODD, D_MODEL, ODD_IN), D_MODEL ** -0.5),
        "pool_w": nrm((N_ODD, POOL_GROUPS, POOL_GC, POOL_GC), POOL_GC ** -0.5),
        "pool_scale": 1.0 + nrm((N_ODD, C_WIDTH), 0.1),
        "w_out_odd": nrm((N_ODD, ODD_MIX, D_MODEL), ODD_MIX ** -0.5),
        "w_ca_q": nrm((DEPTH, D_MODEL, CA_HEADS * CA_HEAD_DIM), D_MODEL ** -0.5),
        "w_ca_kv": nrm((DEPTH, D_MODEL, 2 * CA_HEADS * CA_HEAD_DIM), D_MODEL ** -0.5),
        "w_ca_o": nrm((DEPTH, CA_HEADS * CA_HEAD_DIM, D_MODEL), (CA_HEADS * CA_HEAD_DIM) ** -0.5),
        "w_ffn_in": nrm((DEPTH, D_MODEL, 2 * FFN_HIDDEN), D_MODEL ** -0.5),
        "w_ffn_out": nrm((DEPTH, FFN_HIDDEN, D_MODEL), FFN_HIDDEN ** -0.5),
    }


def reference(x_prompt, x_sample, mem_prompt, cache_conv, cache_swa_k, cache_swa_v, state_pool,
              cache_dil_k, cache_dil_v, cache_mem_k, cache_mem_v, norm_mix, norm_ca, norm_mem,
              norm_ffn, norm_final, w_in_even, conv_w, sinks, w_out_even, w_in_odd, pool_w,
              pool_scale, w_out_odd, w_ca_q, w_ca_kv, w_ca_o, w_ffn_in, w_ffn_out):
    xp, xs = x_prompt, x_sample
    conv_p, swak_p, swav_p, pool_p, dilk_p, dilv_p, memk_p, memv_p = [], [], [], [], [], [], [], []
    conv_s, swak_s, swav_s, pool_s, dilk_s, dilv_s = [], [], [], [], [], []
    for l in range(DEPTH):
        hp = rms_norm(xp, norm_mix[l])
        hs = rms_norm(xs, norm_mix[l])
        if l % 2 == 0:
            e = l // 2
            op, cp, kp, vp = even_mixer_prompt(hp, w_in_even[e], conv_w[e], sinks[e], w_out_even[e])
            os_, cs_, ks_, vs_ = even_mixer_sample(hs, cache_conv[e], cache_swa_k[e], cache_swa_v[e],
                                                  w_in_even[e], conv_w[e], sinks[e], w_out_even[e])
            conv_p.append(cp); swak_p.append(kp); swav_p.append(vp)
            conv_s.append(cs_); swak_s.append(ks_); swav_s.append(vs_)
        else:
            o = l // 2
            op, cp, kp, vp = odd_mixer_prompt(hp, w_in_odd[o], pool_w[o], pool_scale[o], w_out_odd[o])
            os_, cs_, ks_, vs_ = odd_mixer_sample(hs, state_pool[o], cache_dil_k[o], cache_dil_v[o],
                                                 w_in_odd[o], pool_w[o], pool_scale[o], w_out_odd[o])
            pool_p.append(cp); dilk_p.append(kp); dilv_p.append(vp)
            pool_s.append(cs_); dilk_s.append(ks_); dilv_s.append(vs_)
        xp = xp + op
        xs = xs + os_
        mk, mv = memory_kv(mem_prompt, norm_mem[l], w_ca_kv[l])
        memk_p.append(mk); memv_p.append(mv)
        xp = xp + cross_attn(rms_norm(xp, norm_ca[l]), mk, mv, w_ca_q[l], w_ca_o[l])
        xs = xs + cross_attn(rms_norm(xs, norm_ca[l]), cache_mem_k[l], cache_mem_v[l], w_ca_q[l], w_ca_o[l])
        xp = xp + swiglu(rms_norm(xp, norm_ffn[l]), w_ffn_in[l], w_ffn_out[l])
        xs = xs + swiglu(rms_norm(xs, norm_ffn[l]), w_ffn_in[l], w_ffn_out[l])
    y_prompt = rms_norm(xp, norm_final)
    y_sample = rms_norm(xs, norm_final)
    return (y_prompt, y_sample,
            jnp.stack(conv_p), jnp.stack(swak_p), jnp.stack(swav_p),
            jnp.stack(pool_p), jnp.stack(dilk_p), jnp.stack(dilv_p),
            jnp.stack(memk_p), jnp.stack(memv_p),
            jnp.stack(conv_s), jnp.stack(swak_s), jnp.stack(swav_s),
            jnp.stack(pool_s), jnp.stack(dilk_s), jnp.stack(dilv_s))
```

```python
import functools

import jax
import jax.numpy as jnp
from jax import lax
from jax.experimental import pallas as pl
from jax.experimental.pallas import tpu as pltpu

F32 = jnp.float32
BF16 = jnp.bfloat16

LANES = 128
SUBLANES = 8
VMEM_LIMIT = 56 * 1024 * 1024

HEAD_DIM = 64
BLK = 128
CONV_W = 3
POOL_WINDOWS = (2, 4, 8, 16)
POOL_PAD = 15
POOL_GC = 128
DIL_PATTERNS = ((128, 1), (512, 4), (2048, 16))
CA_HEADS = 4
PAST_LEN = 8192
RMS_EPS = 1e-6
NEG_INF = -1e30


def _cparams(*sem):
    return pltpu.CompilerParams(dimension_semantics=sem, vmem_limit_bytes=VMEM_LIMIT)


def _rms(x, g):
    ms = jnp.mean(x * x, axis=-1, keepdims=True)
    return x * lax.rsqrt(ms + RMS_EPS) * g


def _dot(a, b):
    return jnp.dot(a, b, preferred_element_type=F32)


def _dot_nt(a, b):
    return lax.dot_general(a, b, (((1,), (1,)), ((), ())), preferred_element_type=F32)


def _resident(shape):
    return pl.BlockSpec(shape, lambda *_: (0,) * len(shape), pipeline_mode=pl.Buffered(1))


def _norm_matmul_body(x_ref, g_ref, w_ref, o_ref):
    h = _rms(x_ref[...], g_ref[...]).astype(BF16)
    o_ref[...] = _dot(h, w_ref[...])


def _norm_matmul(x, g, w, tm):
    m, d = x.shape
    n = w.shape[1]
    return pl.pallas_call(
        _norm_matmul_body,
        grid=(m // tm,),
        in_specs=[pl.BlockSpec((tm, d), lambda i: (i, 0)), _resident((1, d)), _resident((d, n))],
        out_specs=pl.BlockSpec((tm, n), lambda i: (i, 0)),
        out_shape=jax.ShapeDtypeStruct((m, n), F32),
        compiler_params=_cparams("parallel"),
        name="norm_matmul",
    )(x, g.reshape(1, d), w)


def _matmul_residual_body(*refs, nparts):
    x_ref, o_ref = refs[0], refs[-1]
    acc = x_ref[...]
    for i in range(nparts):
        acc = acc + _dot(refs[1 + 2 * i][...].astype(BF16), refs[2 + 2 * i][...])
    o_ref[...] = acc


def _matmul_residual(x, parts, tm):
    m, d = x.shape
    in_specs = [pl.BlockSpec((tm, d), lambda i: (i, 0))]
    args = [x]
    for a, w in parts:
        in_specs += [pl.BlockSpec((tm, a.shape[1]), lambda i: (i, 0)), _resident(w.shape)]
        args += [a, w]
    return pl.pallas_call(
        functools.partial(_matmul_residual_body, nparts=len(parts)),
        grid=(m // tm,),
        in_specs=in_specs,
        out_specs=pl.BlockSpec((tm, d), lambda i: (i, 0)),
        out_shape=jax.ShapeDtypeStruct((m, d), F32),
        compiler_params=_cparams("parallel"),
        name="matmul_residual",
    )(*args)


FFN_CHUNK = 256


def _swiglu_body(*refs, hidden, final):
    if final:
        x_ref, g_ref, win_ref, wout_ref, gf_ref, o_ref = refs
    else:
        x_ref, g_ref, win_ref, wout_ref, o_ref = refs
    x = x_ref[...]
    h = _rms(x, g_ref[...]).astype(BF16)
    acc = x
    for c in range(hidden // FFN_CHUNK):
        lo = c * FFN_CHUNK
        gate = _dot(h, win_ref[:, lo:lo + FFN_CHUNK])
        up = _dot(h, win_ref[:, hidden + lo:hidden + lo + FFN_CHUNK])
        act = (gate * (1.0 / (1.0 + jnp.exp(-gate))) * up).astype(BF16)
        acc = acc + _dot(act, wout_ref[lo:lo + FFN_CHUNK, :])
    o_ref[...] = _rms(acc, gf_ref[...]) if final else acc


def _swiglu(x, g, win, wout, tm, g_final=None):
    m, d = x.shape
    hidden = wout.shape[0]
    final = g_final is not None
    in_specs = [pl.BlockSpec((tm, d), lambda i: (i, 0)), _resident((1, d)), _resident(win.shape), _resident(wout.shape)]
    args = [x, g.reshape(1, d), win, wout]
    if final:
        in_specs.append(_resident((1, d)))
        args.append(g_final.reshape(1, d))
    return pl.pallas_call(
        functools.partial(_swiglu_body, hidden=hidden, final=final),
        grid=(m // tm,),
        in_specs=in_specs,
        out_specs=pl.BlockSpec((tm, d), lambda i: (i, 0)),
        out_shape=jax.ShapeDtypeStruct((m, d), F32),
        compiler_params=_cparams("parallel"),
        name="swiglu",
    )(*args)


def _conv_prompt_body(p_ref, cw_ref, ya_ref, tail_ref, ubuf, *, tl, width):
    @pl.when(pl.program_id(1) == 0)
    def _():
        ubuf[0:SUBLANES, :] = jnp.zeros((SUBLANES, width), F32)

    bg = p_ref[0, :, 0:width]
    u = p_ref[0, :, width:2 * width] * p_ref[0, :, 2 * width:3 * width]
    ubuf[SUBLANES:SUBLANES + tl, :] = u
    conv = (cw_ref[0:1, :] * ubuf[SUBLANES - 2:SUBLANES - 2 + tl, :]
            + cw_ref[1:2, :] * ubuf[SUBLANES - 1:SUBLANES - 1 + tl, :]
            + cw_ref[2:3, :] * u)
    ya_ref[0] = (bg * conv).astype(BF16)
    tail = ubuf[tl:tl + SUBLANES, :]
    tail_ref[0] = tail
    ubuf[0:SUBLANES, :] = tail


def _conv_prompt(proj, cw, tl):
    b, l, _ = proj.shape
    width = cw.shape[1]
    return pl.pallas_call(
        functools.partial(_conv_prompt_body, tl=tl, width=width),
        grid=(b, l // tl),
        in_specs=[pl.BlockSpec((1, tl, 3 * width), lambda i, j: (i, j, 0)), _resident(cw.shape)],
        out_specs=[pl.BlockSpec((1, tl, width), lambda i, j: (i, j, 0)),
                   pl.BlockSpec((1, SUBLANES, width), lambda i, j: (i, 0, 0))],
        out_shape=[jax.ShapeDtypeStruct((b, l, width), BF16), jax.ShapeDtypeStruct((b, SUBLANES, width), F32)],
        scratch_shapes=[pltpu.VMEM((tl + SUBLANES, width), F32)],
        compiler_params=_cparams("parallel", "arbitrary"),
        name="conv_prompt",
    )(proj, cw)


POOL_CARRY = 16


def _pool_prompt_body(p_ref, pw_ref, sc_ref, y_ref, ubuf, *, tl, width):
    @pl.when(pl.program_id(1) == 0)
    def _():
        ubuf[0:POOL_CARRY, :] = jnp.zeros((POOL_CARRY, width), F32)

    u = p_ref[0]
    ubuf[POOL_CARRY:POOL_CARRY + tl, :] = u
    pos = pl.program_id(1) * tl + lax.broadcasted_iota(jnp.int32, (tl, 1), 0)
    outs = []
    for g, w in enumerate(POOL_WINDOWS):
        sl = slice(g * POOL_GC, (g + 1) * POOL_GC)
        ug = u[:, sl]
        acc = ug
        for i in range(1, w):
            acc = acc + ubuf[POOL_CARRY - i:POOL_CARRY - i + tl, sl]
        cnt = jnp.minimum(pos + 1, w).astype(F32)
        dlt = (acc / cnt - ug).astype(BF16)
        outs.append(_dot(dlt, pw_ref[g]))
    y_ref[0] = (jnp.concatenate(outs, axis=1) * sc_ref[...]).astype(BF16)
    ubuf[0:POOL_CARRY, :] = ubuf[tl:tl + POOL_CARRY, :]


def _pool_prompt(proj, pw, scale, tl):
    b, l, _ = proj.shape
    width = scale.shape[0]
    return pl.pallas_call(
        functools.partial(_pool_prompt_body, tl=tl, width=width),
        grid=(b, l // tl),
        in_specs=[pl.BlockSpec((1, tl, width), lambda i, j: (i, j, 0)), _resident(pw.shape), _resident((1, width))],
        out_specs=pl.BlockSpec((1, tl, width), lambda i, j: (i, j, 0)),
        out_shape=jax.ShapeDtypeStruct((b, l, width), BF16),
        scratch_shapes=[pltpu.VMEM((tl + POOL_CARRY, width), F32)],
        compiler_params=_cparams("parallel", "arbitrary"),
        name="pool_prompt",
    )(proj, pw, scale.reshape(1, width))


def _band_attn_body(*refs, branches, gqa, sink, seq):
    nbr = len(branches)
    if sink:
        q_ref, k_ref, v_ref, sink_ref, y_ref = refs[:5]
        scr = refs[5:]
    else:
        q_ref, k_ref, v_ref, y_ref = refs[:4]
        scr = refs[4:]
    o_scr, l_scr = scr[:nbr], scr[nbr:2 * nbr]
    bias_first, bias_band = scr[2 * nbr], scr[2 * nbr + 1]
    lane = lax.broadcasted_iota(jnp.int32, (1, LANES), 1)
    lo = lane < HEAD_DIM

    rr = lax.broadcasted_iota(jnp.int32, (2 * BLK, 2 * BLK), 0) & (BLK - 1)
    cc = lax.broadcasted_iota(jnp.int32, (2 * BLK, 2 * BLK), 1)
    bias_band[...] = jnp.where((cc >= rr) & (cc <= rr + BLK), 0.0, NEG_INF).astype(F32)
    rr1 = lax.broadcasted_iota(jnp.int32, (2 * BLK, BLK), 0) & (BLK - 1)
    cc1 = lax.broadcasted_iota(jnp.int32, (2 * BLK, BLK), 1)
    bias_first[...] = jnp.where(cc1 <= rr1, 0.0, NEG_INF).astype(F32)

    if gqa:
        kx, vx = scr[2 * nbr + 2], scr[2 * nbr + 3]
        keep = (lane // HEAD_DIM) == (pl.program_id(1) // 2)
        chunk = 2 * BLK
        for c in range(seq // chunk):
            rows = slice(c * chunk, (c + 1) * chunk)
            kb = k_ref[0, rows, :]
            vb = v_ref[0, rows, :]
            kx[rows, :] = jnp.where(keep, kb, pltpu.roll(kb, HEAD_DIM, axis=1))
            vx[rows, :] = jnp.where(keep, vb, pltpu.roll(vb, HEAD_DIM, axis=1))
        load_k = lambda idx: kx[idx, :]
        load_v = lambda idx: vx[idx, :]
    else:
        load_k = lambda idx: k_ref[0, idx, :]
        load_v = lambda idx: v_ref[0, idx, :]
    load_q = lambda idx: q_ref[0, idx, :]

    scale = HEAD_DIM ** -0.5

    def tile(q_t, k_t, v_t, bias):
        qs = q_t * scale
        qm = jnp.concatenate([jnp.where(lo, qs, 0.0), jnp.where(lo, 0.0, qs)], axis=0).astype(BF16)
        s = _dot_nt(qm, k_t.astype(BF16)) + bias
        m = jnp.max(s, axis=1, keepdims=True)
        p = jnp.exp(s - m)
        den = jnp.sum(p, axis=1, keepdims=True)
        o2 = _dot(p.astype(BF16), v_t.astype(BF16)) / den
        lse = jnp.broadcast_to(m + jnp.log(den), (2 * BLK, LANES))
        o = jnp.where(lo, o2[:BLK], o2[BLK:])
        return o, jnp.where(lo, lse[:BLK], lse[BLK:])

    for bi, (w, d) in enumerate(branches):
        nblk = seq // (d * BLK)

        def rows(start, size, d=d):
            if isinstance(start, int):
                return pl.ds(start, size) if d == 1 else pl.ds(start, size, stride=d)
            if d == 1:
                return pl.ds(pl.multiple_of(start, BLK), size)
            return pl.ds(start, size, stride=d)

        def run_class(r, bi=bi, d=d, nblk=nblk, rows=rows):
            idx0 = rows(r, BLK)
            o, lse = tile(load_q(idx0), load_k(idx0), load_v(idx0), bias_first[...])
            o_scr[bi][idx0, :] = o
            l_scr[bi][idx0, :] = lse

            def body(i, carry):
                qi = rows(r + d * BLK * i, BLK)
                ki = rows(r + d * BLK * (i - 1), 2 * BLK)
                o, lse = tile(load_q(qi), load_k(ki), load_v(ki), bias_band[...])
                o_scr[bi][qi, :] = o
                l_scr[bi][qi, :] = lse
                return carry

            if nblk > 1:
                lax.fori_loop(1, nblk, body, 0)

        if d == 1:
            run_class(0)
        else:
            lax.fori_loop(0, d, lambda r, c, run_class=run_class: (run_class(r), c)[1], 0)

    chunk = 2 * BLK
    for c in range(seq // chunk):
        rows_c = slice(c * chunk, (c + 1) * chunk)
        if sink:
            lse = l_scr[0][rows_c, :]
            gate = 1.0 / (1.0 + jnp.exp(sink_ref[...] - lse))
            y = o_scr[0][rows_c, :] * gate
        else:
            lses = [l_scr[bi][rows_c, :] for bi in range(nbr)]
            top = functools.reduce(jnp.maximum, lses)
            es = [jnp.exp(x - top) for x in lses]
            y = sum(e * o_scr[bi][rows_c, :] for bi, e in enumerate(es)) / sum(es)
        y_ref[0, rows_c, :] = y.astype(BF16)


def _band_attn(proj, q_blk, k_blk, v_blk, branches, gqa, sink_row=None):
    b, seq, _ = proj.shape
    nlb = 4
    nbr = len(branches)
    sink = sink_row is not None
    blk = (1, seq, LANES)
    kv_map = (lambda off: (lambda i, j: (i, 0, off))) if gqa else (lambda off: (lambda i, j: (i, 0, off + j)))
    in_specs = [pl.BlockSpec(blk, lambda i, j: (i, 0, q_blk + j)), pl.BlockSpec(blk, kv_map(k_blk)),
                pl.BlockSpec(blk, kv_map(v_blk))]
    args = [proj, proj, proj]
    if sink:
        in_specs.append(pl.BlockSpec((1, LANES), lambda i, j: (0, j)))
        args.append(sink_row)
    scratch = [pltpu.VMEM((seq, LANES), F32) for _ in range(2 * nbr)]
    scratch += [pltpu.VMEM((2 * BLK, BLK), F32), pltpu.VMEM((2 * BLK, 2 * BLK), F32)]
    if gqa:
        scratch += [pltpu.VMEM((seq, LANES), F32), pltpu.VMEM((seq, LANES), F32)]
    return pl.pallas_call(
        functools.partial(_band_attn_body, branches=branches, gqa=gqa, sink=sink, seq=seq),
        grid=(b, nlb),
        in_specs=in_specs,
        out_specs=pl.BlockSpec(blk, lambda i, j: (i, 0, j)),
        out_shape=jax.ShapeDtypeStruct((b, seq, nlb * LANES), BF16),
        scratch_shapes=scratch,
        compiler_params=_cparams("parallel", "parallel"),
        name="band_attn",
    )(*args)


def _cross_prompt_body(x_ref, g_ref, wq_ref, mkv_ref, wo_ref, o_ref, *, heads):
    x = x_ref[0]
    d = x.shape[1]
    dh = d // heads
    h = _rms(x, g_ref[...]).astype(BF16)
    q = (_dot(h, wq_ref[...]) * (dh ** -0.5)).astype(BF16)
    outs = []
    for c in range(heads):
        kc = mkv_ref[0, :, c * dh:(c + 1) * dh].astype(BF16)
        vc = mkv_ref[0, :, d + c * dh:d + (c + 1) * dh].astype(BF16)
        s = _dot_nt(q[:, c * dh:(c + 1) * dh], kc)
        m = jnp.max(s, axis=1, keepdims=True)
        p = jnp.exp(s - m)
        den = jnp.sum(p, axis=1, keepdims=True)
        outs.append((_dot(p.astype(BF16), vc) / den).astype(BF16))
    o_ref[0] = x + _dot(jnp.concatenate(outs, axis=1), wo_ref[...])


def _cross_prompt(x, g, wq, mkv, wo, tl):
    b, l, d = x.shape
    mlen = mkv.shape[1]
    return pl.pallas_call(
        functools.partial(_cross_prompt_body, heads=CA_HEADS),
        grid=(b, l // tl),
        in_specs=[pl.BlockSpec((1, tl, d), lambda i, j: (i, j, 0)), _resident((1, d)), _resident(wq.shape),
                  pl.BlockSpec((1, mlen, 2 * d), lambda i, j: (i, 0, 0)), _resident(wo.shape)],
        out_specs=pl.BlockSpec((1, tl, d), lambda i, j: (i, j, 0)),
        out_shape=jax.ShapeDtypeStruct((b, l, d), F32),
        compiler_params=_cparams("parallel", "parallel"),
        name="cross_prompt",
    )(x, g.reshape(1, d), wq, mkv, wo)


def _decode_attn_body(*refs, G, W, bt, pieces, segments, has_new, epilogue, scale):
    it = iter(refs)
    q_ref, e_ref, et_ref = next(it), next(it), next(it)
    sink_ref = next(it) if epilogue == "sink" else None
    knew_ref, vnew_ref = (next(it), next(it)) if has_new else (None, None)
    k_refs = [next(it) for _ in pieces]
    v_refs = [next(it) for _ in pieces]
    o_ref = next(it)
    pn_scr = next(it) if has_new else None
    e_mat, et_mat = e_ref[...], et_ref[...]

    def gather(prefs, b):
        tiles = []
        for seg in segments:
            for pi in seg:
                for s in range(pieces[pi][1]):
                    tiles.append(prefs[pi][b, :, s * W:(s + 1) * W])
        return tiles[0] if len(tiles) == 1 else jnp.concatenate(tiles, axis=0)

    seg_rows = [sum(pieces[pi][0] * pieces[pi][1] for pi in seg) for seg in segments]

    for g in range(G):
        qg = q_ref[g] * scale
        snew = _dot((knew_ref[...] * qg).astype(BF16), e_mat) if has_new else None
        for b in range(bt):
            qb = qg[b:b + 1, :]
            sc = _dot((gather(k_refs, b) * qb).astype(BF16), e_mat)
            stats, off = [], 0
            for nrows in seg_rows:
                s_ = sc[off:off + nrows]
                off += nrows
                m = jnp.max(s_, axis=0, keepdims=True)
                if has_new:
                    sn = snew[b:b + 1, :]
                    m = jnp.maximum(m, sn)
                p = jnp.exp(s_ - m)
                den = jnp.sum(p, axis=0, keepdims=True)
                pn = None
                if has_new:
                    pn = jnp.exp(sn - m)
                    den = den + pn
                stats.append((p, pn, den, m + jnp.log(den)))
            if epilogue == "combine":
                top = functools.reduce(jnp.maximum, [st[3] for st in stats])
                es = [jnp.exp(st[3] - top) for st in stats]
                tot = sum(es)
                facs = [e / (tot * st[2]) for e, st in zip(es, stats)]
            elif epilogue == "sink":
                facs = [1.0 / ((1.0 + jnp.exp(sink_ref[g] - st[3])) * st[2]) for st in stats]
            else:
                facs = [1.0 / st[2] for st in stats]
            pf = [st[0] * f for st, f in zip(stats, facs)]
            pf = pf[0] if len(pf) == 1 else jnp.concatenate(pf, axis=0)
            pe = _dot(pf.astype(BF16), et_mat)
            o_ref[g, b:b + 1, :] = jnp.sum(pe * gather(v_refs, b), axis=0, keepdims=True)
            if has_new:
                pn_scr[b:b + 1, :] = sum(st[1] * f for st, f in zip(stats, facs))
        if has_new:
            o_ref[g] = o_ref[g] + _dot(pn_scr[...].astype(BF16), et_mat) * vnew_ref[...]


def _decode_attn(q, kv_pieces, segments, n_heads, head_dim, bt, epilogue="none", sink=None, new=None):
    G, bs, W = q.shape
    head = jnp.arange(W, dtype=jnp.int32) // head_dim
    e_mat = (head[:, None] == jnp.arange(LANES, dtype=jnp.int32)[None, :]).astype(BF16)
    et_mat = e_mat.T
    in_specs = [pl.BlockSpec((G, bt, W), lambda i: (0, i, 0)), _resident(e_mat.shape), _resident(et_mat.shape)]
    args = [q, e_mat, et_mat]
    if epilogue == "sink":
        in_specs.append(_resident(sink.shape))
        args.append(sink)
    has_new = new is not None
    if has_new:
        in_specs += [pl.BlockSpec((bt, W), lambda i: (i, 0))] * 2
        args += list(new)
    pieces = []
    for which in (0, 1):
        for piece in kv_pieces:
            in_specs.append(pl.BlockSpec(piece[2], piece[3]))
            args.append(piece[which])
    for piece in kv_pieces:
        pieces.append((piece[2][1], piece[4]))
    scratch = [pltpu.VMEM((bt, LANES), F32)] if has_new else []
    return pl.pallas_call(
        functools.partial(_decode_attn_body, G=G, W=W, bt=bt, pieces=tuple(pieces), segments=segments,
                          has_new=has_new, epilogue=epilogue, scale=head_dim ** -0.5),
        grid=(bs // bt,),
        in_specs=in_specs,
        out_specs=pl.BlockSpec((G, bt, W), lambda i: (0, i, 0)),
        out_shape=jax.ShapeDtypeStruct((G, bs, W), F32),
        scratch_shapes=scratch,
        compiler_params=_cparams("parallel"),
        name="decode_attn",
    )(*args)


def _conv_sample_body(p_ref, b0_ref, b1_ref, cw_ref, ya_ref, u_ref, *, width):
    bg = p_ref[:, 0:width]
    u = p_ref[:, width:2 * width] * p_ref[:, 2 * width:3 * width]
    conv = cw_ref[0:1, :] * b0_ref[...] + cw_ref[1:2, :] * b1_ref[...] + cw_ref[2:3, :] * u
    ya_ref[...] = (bg * conv).astype(BF16)
    u_ref[...] = u


def _conv_sample(proj, buf0, buf1, cw):
    bs = proj.shape[0]
    width = cw.shape[1]
    full = lambda shape: pl.BlockSpec(shape, lambda i: (0,) * len(shape))
    return pl.pallas_call(
        functools.partial(_conv_sample_body, width=width),
        grid=(1,),
        in_specs=[full((bs, 3 * width)), full((bs, width)), full((bs, width)), full(cw.shape)],
        out_specs=[full((bs, width)), full((bs, width))],
        out_shape=[jax.ShapeDtypeStruct((bs, width), BF16), jax.ShapeDtypeStruct((bs, width), F32)],
        compiler_params=_cparams("arbitrary"),
        name="conv_sample",
    )(proj, buf0, buf1, cw)


def _pool_sample_body(p_ref, buf_ref, pw_ref, sc_ref, y_ref):
    u = p_ref[...]
    outs = []
    for g, w in enumerate(POOL_WINDOWS):
        sl = slice(g * POOL_GC, (g + 1) * POOL_GC)
        ug = u[:, sl]
        acc = ug
        for i in range(1, w):
            acc = acc + buf_ref[POOL_PAD - i][:, sl]
        cnt = float(min(PAST_LEN + 1, w))
        dlt = (acc / cnt - ug).astype(BF16)
        outs.append(_dot(dlt, pw_ref[g]))
    y_ref[...] = (jnp.concatenate(outs, axis=1) * sc_ref[...]).astype(BF16)


def _pool_sample(proj, buf_t, pw, scale):
    bs = proj.shape[0]
    width = scale.shape[0]
    full = lambda shape: pl.BlockSpec(shape, lambda i: (0,) * len(shape))
    return pl.pallas_call(
        _pool_sample_body,
        grid=(1,),
        in_specs=[full((bs, width)), full(buf_t.shape), full(pw.shape), full((1, width))],
        out_specs=full((bs, width)),
        out_shape=jax.ShapeDtypeStruct((bs, width), BF16),
        compiler_params=_cparams("arbitrary"),
        name="pool_sample",
    )(proj, buf_t, pw, scale.reshape(1, width))


def kernel(x_prompt, x_sample, mem_prompt, cache_conv, cache_swa_k, cache_swa_v, state_pool, cache_dil_k, cache_dil_v, cache_mem_k, cache_mem_v, norm_mix, norm_ca, norm_mem, norm_ffn, norm_final, w_in_even, conv_w, sinks, w_out_even, w_in_odd, pool_w, pool_scale, w_out_odd, w_ca_q, w_ca_kv, w_ca_o, w_ffn_in, w_ffn_out):
    B, L, D = x_prompt.shape
    Bs = x_sample.shape[0]
    depth = norm_mix.shape[0]
    mlen = mem_prompt.shape[1]
    A = conv_w.shape[2]
    HQ = D // 2
    n_q = HQ // HEAD_DIM
    n_kv = cache_swa_k.shape[3]
    KV = n_kv * HEAD_DIM
    TM = 512
    TL = 512

    xp = x_prompt.reshape(B * L, D)
    xs = x_sample.reshape(Bs, D)
    mem = mem_prompt.reshape(B * mlen, D)

    conv_p, swak_p, swav_p, pool_p, dilk_p, dilv_p, memk_p, memv_p = [], [], [], [], [], [], [], []
    conv_s, swak_s, swav_s, pool_s, dilk_s, dilv_s = [], [], [], [], [], []

    for l in range(depth):
        if l % 2 == 0:
            e = l // 2
            w_in = w_in_even[e].astype(BF16)
            w_out = w_out_even[e].astype(BF16)
            q0, k0, v0 = 3 * A, 3 * A + HQ, 3 * A + HQ + KV
            proj = _norm_matmul(xp, norm_mix[l], w_in, TM).reshape(B, L, -1)
            ya, tail = _conv_prompt(proj, conv_w[e], TL)
            sink_row = jnp.repeat(sinks[e], HEAD_DIM).reshape(1, HQ)
            yb = _band_attn(proj, q0 // LANES, k0 // LANES, v0 // LANES, ((BLK, 1),), True, sink_row)
            xp = _matmul_residual(xp, [(ya.reshape(B * L, A), w_out[:A]), (yb.reshape(B * L, HQ), w_out[A:])], TM)
            conv_p.append(tail[:, SUBLANES - (CONV_W - 1):])
            swak_p.append(proj[:, L - BLK:, k0:k0 + KV].reshape(B, BLK, n_kv, HEAD_DIM))
            swav_p.append(proj[:, L - BLK:, v0:v0 + KV].reshape(B, BLK, n_kv, HEAD_DIM))
            proj_s = _norm_matmul(xs, norm_mix[l], w_in, Bs)
            ya_s, u_s = _conv_sample(proj_s, cache_conv[e][:, 0], cache_conv[e][:, 1], conv_w[e])
            conv_s.append(jnp.stack([cache_conv[e][:, 1], u_s], axis=1))
            k_new, v_new = proj_s[:, k0:k0 + KV], proj_s[:, v0:v0 + KV]
            rep = n_q // n_kv
            q_r = proj_s[:, q0:q0 + HQ].reshape(Bs, n_kv, rep, HEAD_DIM).transpose(2, 0, 1, 3).reshape(rep, Bs, KV)
            sink_r = jnp.zeros((rep, 1, LANES), F32).at[:, 0, :n_kv].set(sinks[e].reshape(n_kv, rep).T)
            nbuf = cache_swa_k.shape[2]
            piece = (cache_swa_k[e].reshape(Bs, nbuf, KV), cache_swa_v[e].reshape(Bs, nbuf, KV),
                     (8, nbuf, KV), lambda i: (i, 0, 0), 1)
            o_r = _decode_attn(q_r, [piece], ((0,),), n_kv, HEAD_DIM, 8, "sink", sink_r, (k_new, v_new))
            yb_s = o_r.reshape(rep, Bs, n_kv, HEAD_DIM).transpose(1, 2, 0, 3).reshape(Bs, HQ)
            xs = _matmul_residual(xs, [(ya_s, w_out[:A]), (yb_s, w_out[A:])], Bs)
            swak_s.append(k_new.reshape(Bs, 1, n_kv, HEAD_DIM))
            swav_s.append(v_new.reshape(Bs, 1, n_kv, HEAD_DIM))
        else:
            o = l // 2
            w_in = w_in_odd[o].astype(BF16)
            w_out = w_out_odd[o].astype(BF16)
            q0, k0, v0 = A, A + HQ, A + 2 * HQ
            proj = _norm_matmul(xp, norm_mix[l], w_in, TM).reshape(B, L, -1)
            yc = _pool_prompt(proj, pool_w[o].astype(BF16), pool_scale[o], TL)
            yd = _band_attn(proj, q0 // LANES, k0 // LANES, v0 // LANES, DIL_PATTERNS, False)
            xp = _matmul_residual(xp, [(yc.reshape(B * L, A), w_out[:A]), (yd.reshape(B * L, HQ), w_out[A:])], TM)
            pool_p.append(proj[:, L - POOL_PAD:, :A])
            dilk_p.append(proj[:, :, k0:k0 + HQ].reshape(B, L, n_q, HEAD_DIM))
            dilv_p.append(proj[:, :, v0:v0 + HQ].reshape(B, L, n_q, HEAD_DIM))
            proj_s = _norm_matmul(xs, norm_mix[l], w_in, Bs)
            yc_s = _pool_sample(proj_s, state_pool[o].transpose(1, 0, 2), pool_w[o].astype(BF16), pool_scale[o])
            pool_s.append(jnp.concatenate([state_pool[o][:, 1:], proj_s[:, None, :A]], axis=1))
            k_new, v_new = proj_s[:, k0:k0 + HQ], proj_s[:, v0:v0 + HQ]
            nbuf = cache_dil_k.shape[2]
            big = DIL_PATTERNS[-1][1]
            kc = cache_dil_k[o].reshape(Bs, nbuf // big, big * HQ)
            vc = cache_dil_v[o].reshape(Bs, nbuf // big, big * HQ)
            bt = 8
            pieces, segments = [], []
            for w, d in DIL_PATTERNS:
                view_rows = w // big
                rb = (nbuf // big) // view_rows - 1
                if d == 1:
                    lane_pieces = [((bt, view_rows, big * HQ), 0, big)]
                else:
                    lane_pieces = [((bt, view_rows, HQ), c, 1) for c in range(0, big, d)]
                seg = []
                for blk_shape, c, nsub in lane_pieces:
                    seg.append(len(pieces))
                    pieces.append((kc, vc, blk_shape, (lambda rb, c: (lambda i: (i, rb, c)))(rb, c), nsub))
                segments.append(tuple(seg))
            yd_s = _decode_attn(proj_s[:, q0:q0 + HQ][None], pieces, tuple(segments), n_q, HEAD_DIM, bt,
                                "combine", None, (k_new, v_new))[0]
            xs = _matmul_residual(xs, [(yc_s, w_out[:A]), (yd_s, w_out[A:])], Bs)
            dilk_s.append(k_new.reshape(Bs, 1, n_q, HEAD_DIM))
            dilv_s.append(v_new.reshape(Bs, 1, n_q, HEAD_DIM))

        wq = w_ca_q[l].astype(BF16)
        wo = w_ca_o[l].astype(BF16)
        mkv = _norm_matmul(mem, norm_mem[l], w_ca_kv[l].astype(BF16), TM).reshape(B, mlen, 2 * D)
        memk_p.append(mkv[:, :, :D].reshape(B, mlen, CA_HEADS, D // CA_HEADS))
        memv_p.append(mkv[:, :, D:].reshape(B, mlen, CA_HEADS, D // CA_HEADS))
        xp = _cross_prompt(xp.reshape(B, L, D), norm_ca[l], wq, mkv, wo, TL).reshape(B * L, D)
        q_s = _norm_matmul(xs, norm_ca[l], wq, Bs)
        piece = (cache_mem_k[l].reshape(Bs, mlen, D), cache_mem_v[l].reshape(Bs, mlen, D),
                 (8, mlen, D), lambda i: (i, 0, 0), 1)
        o_s = _decode_attn(q_s[None], [piece], ((0,),), CA_HEADS, D // CA_HEADS, 8)[0]
        xs = _matmul_residual(xs, [(o_s, wo)], Bs)

        g_final = norm_final if l == depth - 1 else None
        w_f_in = w_ffn_in[l].astype(BF16)
        w_f_out = w_ffn_out[l].astype(BF16)
        xp = _swiglu(xp, norm_ffn[l], w_f_in, w_f_out, TM, g_final)
        xs = _swiglu(xs, norm_ffn[l], w_f_in, w_f_out, Bs, g_final)

    return (xp.reshape(B, L, D), xs.reshape(Bs, 1, D),
            jnp.stack(conv_p), jnp.stack(swak_p), jnp.stack(swav_p),
            jnp.stack(pool_p), jnp.stack(dilk_p), jnp.stack(dilv_p),
            jnp.stack(memk_p), jnp.stack(memv_p),
            jnp.stack(conv_s), jnp.stack(swak_s), jnp.stack(swav_s),
            jnp.stack(pool_s), jnp.stack(dilk_s), jnp.stack(dilv_s))
```

```python
import functools

import jax
import jax.numpy as jnp
from jax import lax
from jax.experimental import pallas as pl
from jax.experimental.pallas import tpu as pltpu

F32 = jnp.float32
BF16 = jnp.bfloat16

LANES = 128
SUBLANES = 8
VMEM_LIMIT = 56 * 1024 * 1024

HEAD_DIM = 64
BLK = 128
CONV_W = 3
POOL_WINDOWS = (2, 4, 8, 16)
POOL_PAD = 15
POOL_GC = 128
DIL_PATTERNS = ((128, 1), (512, 4), (2048, 16))
CA_HEADS = 4
PAST_LEN = 8192
RMS_EPS = 1e-6
NEG_INF = -1e30


def _cparams(*sem):
    return pltpu.CompilerParams(dimension_semantics=sem, vmem_limit_bytes=VMEM_LIMIT)


def _rms(x, g):
    ms = jnp.mean(x * x, axis=-1, keepdims=True)
    return x * lax.rsqrt(ms + RMS_EPS) * g


def _dot(a, b):
    return jnp.dot(a, b, preferred_element_type=F32)


def _dot_nt(a, b):
    return lax.dot_general(a, b, (((1,), (1,)), ((), ())), preferred_element_type=F32)


def _resident(shape):
    return pl.BlockSpec(shape, lambda *_: (0,) * len(shape), pipeline_mode=pl.Buffered(1))


def _layer_block(stack, layer, rows=None, row_block=0):
    _, k, n = stack.shape
    return pl.BlockSpec((None, k if rows is None else rows, n), lambda *_: (layer, row_block, 0),
                        pipeline_mode=pl.Buffered(1))


def _norm_matmul_body(x_ref, g_ref, w_ref, o_ref):
    h = _rms(x_ref[...], g_ref[...]).astype(BF16)
    o_ref[...] = _dot(h, w_ref[...])


def _norm_matmul(x, g, w, layer, tm):
    m, d = x.shape
    n = w.shape[2]
    return pl.pallas_call(
        _norm_matmul_body,
        grid=(m // tm,),
        in_specs=[pl.BlockSpec((tm, d), lambda i: (i, 0)), _resident((1, d)), _layer_block(w, layer)],
        out_specs=pl.BlockSpec((tm, n), lambda i: (i, 0)),
        out_shape=jax.ShapeDtypeStruct((m, n), F32),
        compiler_params=_cparams("parallel"),
        name="norm_matmul",
    )(x, g.reshape(1, d), w)


def _norm_matmul_kvt_body(x_ref, g_ref, w_ref, o_ref, kt_ref, vt_ref, *, k0, v0, width):
    h = _rms(x_ref[0], g_ref[...]).astype(BF16)
    r = _dot(h, w_ref[...])
    o_ref[0] = r
    kt_ref[0] = r[:, k0:k0 + width].T
    vt_ref[0] = r[:, v0:v0 + width].T


def _norm_matmul_kvt(x, g, w, layer, tm, k0, v0, width):
    b, l, d = x.shape
    n = w.shape[2]
    t_spec = pl.BlockSpec((1, width, tm), lambda i, j: (i, 0, j))
    return pl.pallas_call(
        functools.partial(_norm_matmul_kvt_body, k0=k0, v0=v0, width=width),
        grid=(b, l // tm),
        in_specs=[pl.BlockSpec((1, tm, d), lambda i, j: (i, j, 0)), _resident((1, d)), _layer_block(w, layer)],
        out_specs=[pl.BlockSpec((1, tm, n), lambda i, j: (i, j, 0)), t_spec, t_spec],
        out_shape=[jax.ShapeDtypeStruct((b, l, n), F32), jax.ShapeDtypeStruct((b, width, l), F32),
                   jax.ShapeDtypeStruct((b, width, l), F32)],
        compiler_params=_cparams("parallel", "parallel"),
        name="norm_matmul_kvt",
    )(x, g.reshape(1, d), w)


def _matmul_residual_body(*refs, nparts):
    x_ref, o_ref = refs[0], refs[-1]
    acc = x_ref[...]
    for i in range(nparts):
        acc = acc + _dot(refs[1 + 2 * i][...].astype(BF16), refs[2 + 2 * i][...])
    o_ref[...] = acc


def _matmul_residual(x, parts, w, layer, tm):
    m, d = x.shape
    in_specs = [pl.BlockSpec((tm, d), lambda i: (i, 0))]
    args = [x]
    for row_block, a in enumerate(parts):
        in_specs += [pl.BlockSpec((tm, a.shape[1]), lambda i: (i, 0)), _layer_block(w, layer, a.shape[1], row_block)]
        args += [a, w]
    return pl.pallas_call(
        functools.partial(_matmul_residual_body, nparts=len(parts)),
        grid=(m // tm,),
        in_specs=in_specs,
        out_specs=pl.BlockSpec((tm, d), lambda i: (i, 0)),
        out_shape=jax.ShapeDtypeStruct((m, d), F32),
        compiler_params=_cparams("parallel"),
        name="matmul_residual",
    )(*args)


FFN_CHUNK = 256


def _swiglu_body(*refs, hidden, final):
    if final:
        x_ref, g_ref, win_ref, wout_ref, gf_ref, o_ref = refs
    else:
        x_ref, g_ref, win_ref, wout_ref, o_ref = refs
    x = x_ref[...]
    h = _rms(x, g_ref[...]).astype(BF16)
    acc = x
    for c in range(hidden // FFN_CHUNK):
        lo = c * FFN_CHUNK
        gate = _dot(h, win_ref[:, lo:lo + FFN_CHUNK])
        up = _dot(h, win_ref[:, hidden + lo:hidden + lo + FFN_CHUNK])
        act = (gate * (1.0 / (1.0 + jnp.exp(-gate))) * up).astype(BF16)
        acc = acc + _dot(act, wout_ref[lo:lo + FFN_CHUNK, :])
    o_ref[...] = _rms(acc, gf_ref[...]) if final else acc


def _swiglu(x, g, win, wout, layer, tm, g_final=None):
    m, d = x.shape
    hidden = wout.shape[1]
    final = g_final is not None
    in_specs = [pl.BlockSpec((tm, d), lambda i: (i, 0)), _resident((1, d)), _layer_block(win, layer),
                _layer_block(wout, layer)]
    args = [x, g.reshape(1, d), win, wout]
    if final:
        in_specs.append(_resident((1, d)))
        args.append(g_final.reshape(1, d))
    return pl.pallas_call(
        functools.partial(_swiglu_body, hidden=hidden, final=final),
        grid=(m // tm,),
        in_specs=in_specs,
        out_specs=pl.BlockSpec((tm, d), lambda i: (i, 0)),
        out_shape=jax.ShapeDtypeStruct((m, d), F32),
        compiler_params=_cparams("parallel"),
        name="swiglu",
    )(*args)


def _conv_prompt_body(p_ref, cw_ref, ya_ref, tail_ref, ubuf, *, tl, width):
    @pl.when(pl.program_id(1) == 0)
    def _():
        ubuf[0:SUBLANES, :] = jnp.zeros((SUBLANES, width), F32)

    bg = p_ref[0, :, 0:width]
    u = p_ref[0, :, width:2 * width] * p_ref[0, :, 2 * width:3 * width]
    ubuf[SUBLANES:SUBLANES + tl, :] = u
    conv = (cw_ref[0:1, :] * ubuf[SUBLANES - 2:SUBLANES - 2 + tl, :]
            + cw_ref[1:2, :] * ubuf[SUBLANES - 1:SUBLANES - 1 + tl, :]
            + cw_ref[2:3, :] * u)
    ya_ref[0] = (bg * conv).astype(BF16)
    tail = ubuf[tl:tl + SUBLANES, :]
    tail_ref[0] = tail
    ubuf[0:SUBLANES, :] = tail


def _conv_prompt(proj, cw, tl):
    b, l, _ = proj.shape
    width = cw.shape[1]
    return pl.pallas_call(
        functools.partial(_conv_prompt_body, tl=tl, width=width),
        grid=(b, l // tl),
        in_specs=[pl.BlockSpec((1, tl, 3 * width), lambda i, j: (i, j, 0)), _resident(cw.shape)],
        out_specs=[pl.BlockSpec((1, tl, width), lambda i, j: (i, j, 0)),
                   pl.BlockSpec((1, SUBLANES, width), lambda i, j: (i, 0, 0))],
        out_shape=[jax.ShapeDtypeStruct((b, l, width), BF16), jax.ShapeDtypeStruct((b, SUBLANES, width), F32)],
        scratch_shapes=[pltpu.VMEM((tl + SUBLANES, width), F32)],
        compiler_params=_cparams("parallel", "arbitrary"),
        name="conv_prompt",
    )(proj, cw)


POOL_CARRY = 16


def _pool_prompt_body(p_ref, pw_ref, sc_ref, y_ref, ubuf, *, tl, width):
    @pl.when(pl.program_id(1) == 0)
    def _():
        ubuf[0:POOL_CARRY, :] = jnp.zeros((POOL_CARRY, width), F32)

    u = p_ref[0]
    ubuf[POOL_CARRY:POOL_CARRY + tl, :] = u
    pos = pl.program_id(1) * tl + lax.broadcasted_iota(jnp.int32, (tl, 1), 0)
    outs = []
    for g, w in enumerate(POOL_WINDOWS):
        sl = slice(g * POOL_GC, (g + 1) * POOL_GC)
        ug = u[:, sl]
        acc = ug
        for i in range(1, w):
            acc = acc + ubuf[POOL_CARRY - i:POOL_CARRY - i + tl, sl]
        cnt = jnp.minimum(pos + 1, w).astype(F32)
        dlt = (acc / cnt - ug).astype(BF16)
        outs.append(_dot(dlt, pw_ref[g]))
    y_ref[0] = (jnp.concatenate(outs, axis=1) * sc_ref[...]).astype(BF16)
    ubuf[0:POOL_CARRY, :] = ubuf[tl:tl + POOL_CARRY, :]


def _pool_prompt(proj, pw, scale, tl):
    b, l, _ = proj.shape
    width = scale.shape[0]
    return pl.pallas_call(
        functools.partial(_pool_prompt_body, tl=tl, width=width),
        grid=(b, l // tl),
        in_specs=[pl.BlockSpec((1, tl, width), lambda i, j: (i, j, 0)), _resident(pw.shape), _resident((1, width))],
        out_specs=pl.BlockSpec((1, tl, width), lambda i, j: (i, j, 0)),
        out_shape=jax.ShapeDtypeStruct((b, l, width), BF16),
        scratch_shapes=[pltpu.VMEM((tl + POOL_CARRY, width), F32)],
        compiler_params=_cparams("parallel", "arbitrary"),
        name="pool_prompt",
    )(proj, pw, scale.reshape(1, width))


TILES_IN_FLIGHT = 5


def _unroll(trips, limit=TILES_IN_FLIGHT):
    return max(u for u in range(1, max(limit, 1) + 1) if trips % u == 0)


def _band_attn_body(*refs, branches, gqa, sink, seq):
    nbr = len(branches)
    if sink:
        q_ref, k_ref, v_ref, sink_ref, y_ref = refs[:5]
        scr = refs[5:]
    else:
        q_ref, k_ref, v_ref, y_ref = refs[:4]
        scr = refs[4:]
    o_scr, l_scr = scr[:nbr], scr[nbr:2 * nbr]
    bias_first, bias_band = scr[2 * nbr], scr[2 * nbr + 1]
    lane = lax.broadcasted_iota(jnp.int32, (1, LANES), 1)
    lo = lane < HEAD_DIM

    rr = lax.broadcasted_iota(jnp.int32, (2 * BLK, 2 * BLK), 0) & (BLK - 1)
    cc = lax.broadcasted_iota(jnp.int32, (2 * BLK, 2 * BLK), 1)
    bias_band[...] = jnp.where((cc >= rr) & (cc <= rr + BLK), 0.0, NEG_INF).astype(F32)
    rr1 = lax.broadcasted_iota(jnp.int32, (2 * BLK, BLK), 0) & (BLK - 1)
    cc1 = lax.broadcasted_iota(jnp.int32, (2 * BLK, BLK), 1)
    bias_first[...] = jnp.where(cc1 <= rr1, 0.0, NEG_INF).astype(F32)

    if gqa:
        kx, vx = scr[2 * nbr + 2], scr[2 * nbr + 3]
        keep = (lane // HEAD_DIM) == (pl.program_id(1) // 2)
        chunk = 2 * BLK
        for c in range(seq // chunk):
            rows = slice(c * chunk, (c + 1) * chunk)
            kb = k_ref[0, rows, :]
            vb = v_ref[0, rows, :]
            kx[rows, :] = jnp.where(keep, kb, pltpu.roll(kb, HEAD_DIM, axis=1))
            vx[rows, :] = jnp.where(keep, vb, pltpu.roll(vb, HEAD_DIM, axis=1))
        load_k = lambda idx: kx[idx, :]
        load_v = lambda idx: vx[idx, :]
    else:
        load_k = lambda idx: k_ref[0, idx, :]
        load_v = lambda idx: v_ref[0, idx, :]
    load_q = lambda idx: q_ref[0, idx, :]

    scale = HEAD_DIM ** -0.5

    def tile(q_t, k_t, v_t, bias):
        qs = q_t * scale
        qm = jnp.concatenate([jnp.where(lo, qs, 0.0), jnp.where(lo, 0.0, qs)], axis=0).astype(BF16)
        s = _dot_nt(qm, k_t.astype(BF16)) + bias
        m = jnp.max(s, axis=1, keepdims=True)
        p = jnp.exp(s - m)
        den = jnp.sum(p, axis=1, keepdims=True)
        o2 = _dot(p.astype(BF16), v_t.astype(BF16)) / den
        lse = jnp.broadcast_to(m + jnp.log(den), (2 * BLK, LANES))
        o = jnp.where(lo, o2[:BLK], o2[BLK:])
        return o, jnp.where(lo, lse[:BLK], lse[BLK:])

    for bi, (w, d) in enumerate(branches):
        nblk = seq // (d * BLK)

        def rows(start, size, d=d):
            if isinstance(start, int):
                return pl.ds(start, size) if d == 1 else pl.ds(start, size, stride=d)
            if d == 1:
                return pl.ds(pl.multiple_of(start, BLK), size)
            return pl.ds(start, size, stride=d)

        def run_class(r, bi=bi, d=d, nblk=nblk, rows=rows):
            idx0 = rows(r, BLK)
            o, lse = tile(load_q(idx0), load_k(idx0), load_v(idx0), bias_first[...])
            o_scr[bi][idx0, :] = o
            l_scr[bi][idx0, :] = lse

            def body(i, carry):
                qi = rows(r + d * BLK * i, BLK)
                ki = rows(r + d * BLK * (i - 1), 2 * BLK)
                o, lse = tile(load_q(qi), load_k(ki), load_v(ki), bias_band[...])
                o_scr[bi][qi, :] = o
                l_scr[bi][qi, :] = lse
                return carry

            if nblk > 1:
                lax.fori_loop(1, nblk, body, 0, unroll=_unroll(nblk - 1))

        if d == 1:
            run_class(0)
        else:
            lax.fori_loop(0, d, lambda r, c, run_class=run_class: (run_class(r), c)[1], 0,
                          unroll=_unroll(d, TILES_IN_FLIGHT // min(nblk, TILES_IN_FLIGHT)))

    chunk = 2 * BLK
    for c in range(seq // chunk):
        rows_c = slice(c * chunk, (c + 1) * chunk)
        if sink:
            lse = l_scr[0][rows_c, :]
            gate = 1.0 / (1.0 + jnp.exp(sink_ref[...] - lse))
            y = o_scr[0][rows_c, :] * gate
        else:
            lses = [l_scr[bi][rows_c, :] for bi in range(nbr)]
            top = functools.reduce(jnp.maximum, lses)
            es = [jnp.exp(x - top) for x in lses]
            y = sum(e * o_scr[bi][rows_c, :] for bi, e in enumerate(es)) / sum(es)
        y_ref[0, rows_c, :] = y.astype(BF16)


def _band_attn(proj, q_blk, k_blk, v_blk, branches, gqa, sink_row=None):
    b, seq, _ = proj.shape
    nlb = 4
    nbr = len(branches)
    sink = sink_row is not None
    blk = (1, seq, LANES)
    kv_map = (lambda off: (lambda i, j: (i, 0, off))) if gqa else (lambda off: (lambda i, j: (i, 0, off + j)))
    in_specs = [pl.BlockSpec(blk, lambda i, j: (i, 0, q_blk + j)), pl.BlockSpec(blk, kv_map(k_blk)),
                pl.BlockSpec(blk, kv_map(v_blk))]
    args = [proj, proj, proj]
    if sink:
        in_specs.append(pl.BlockSpec((1, LANES), lambda i, j: (0, j)))
        args.append(sink_row)
    scratch = [pltpu.VMEM((seq, LANES), F32) for _ in range(2 * nbr)]
    scratch += [pltpu.VMEM((2 * BLK, BLK), F32), pltpu.VMEM((2 * BLK, 2 * BLK), F32)]
    if gqa:
        scratch += [pltpu.VMEM((seq, LANES), F32), pltpu.VMEM((seq, LANES), F32)]
    return pl.pallas_call(
        functools.partial(_band_attn_body, branches=branches, gqa=gqa, sink=sink, seq=seq),
        grid=(b, nlb),
        in_specs=in_specs,
        out_specs=pl.BlockSpec(blk, lambda i, j: (i, 0, j)),
        out_shape=jax.ShapeDtypeStruct((b, seq, nlb * LANES), BF16),
        scratch_shapes=scratch,
        compiler_params=_cparams("parallel", "parallel"),
        name="band_attn",
    )(*args)


def _cross_prompt_body(x_ref, g_ref, wq_ref, mkv_ref, wo_ref, o_ref, *, heads):
    x = x_ref[0]
    d = x.shape[1]
    dh = d // heads
    h = _rms(x, g_ref[...]).astype(BF16)
    q = (_dot(h, wq_ref[...]) * (dh ** -0.5)).astype(BF16)
    outs = []
    for c in range(heads):
        kc = mkv_ref[0, :, c * dh:(c + 1) * dh].astype(BF16)
        vc = mkv_ref[0, :, d + c * dh:d + (c + 1) * dh].astype(BF16)
        s = _dot_nt(q[:, c * dh:(c + 1) * dh], kc)
        m = jnp.max(s, axis=1, keepdims=True)
        p = jnp.exp(s - m)
        den = jnp.sum(p, axis=1, keepdims=True)
        outs.append((_dot(p.astype(BF16), vc) / den).astype(BF16))
    o_ref[0] = x + _dot(jnp.concatenate(outs, axis=1), wo_ref[...])


def _cross_prompt(x, g, wq, mkv, wo, layer, tl):
    b, l, d = x.shape
    mlen = mkv.shape[1]
    return pl.pallas_call(
        functools.partial(_cross_prompt_body, heads=CA_HEADS),
        grid=(b, l // tl),
        in_specs=[pl.BlockSpec((1, tl, d), lambda i, j: (i, j, 0)), _resident((1, d)), _layer_block(wq, layer),
                  pl.BlockSpec((1, mlen, 2 * d), lambda i, j: (i, 0, 0)), _layer_block(wo, layer)],
        out_specs=pl.BlockSpec((1, tl, d), lambda i, j: (i, j, 0)),
        out_shape=jax.ShapeDtypeStruct((b, l, d), F32),
        compiler_params=_cparams("parallel", "parallel"),
        name="cross_prompt",
    )(x, g.reshape(1, d), wq, mkv, wo)


def _decode_t_body(*refs, bt, nbr, sink, diag, scale):
    it = iter(refs)
    q_ref, knew_ref, vnew_ref, bias_ref = next(it), next(it), next(it), next(it)
    sink_ref = next(it) if sink else None
    diag_ref = next(it) if diag else None
    k_ref, v_ref, o_ref = next(it), next(it), next(it)
    for b in range(bt):
        qb = q_ref[b] * scale
        s = _dot(qb.astype(BF16), k_ref[0, b].astype(BF16))
        sn = jnp.sum(qb * knew_ref[b], axis=1, keepdims=True)
        stats = []
        for br in range(nbr):
            s_ = s + bias_ref[br]
            m = jnp.maximum(jnp.max(s_, axis=1, keepdims=True), sn)
            p = jnp.exp(s_ - m)
            pn = jnp.exp(sn - m)
            den = jnp.sum(p, axis=1, keepdims=True) + pn
            stats.append((p, pn, den, m + jnp.log(den)))
        if sink:
            facs = [1.0 / ((1.0 + jnp.exp(sink_ref[:, 0:1] - st[3])) * st[2]) for st in stats]
        else:
            top = functools.reduce(jnp.maximum, [st[3] for st in stats])
            es = [jnp.exp(st[3] - top) for st in stats]
            tot = sum(es)
            facs = [e / (tot * st[2]) for e, st in zip(es, stats)]
        p_all = sum(st[0] * f for st, f in zip(stats, facs))
        pn_all = sum(st[1] * f for st, f in zip(stats, facs))
        o = _dot_nt(p_all.astype(BF16), v_ref[0, b].astype(BF16)) + pn_all * vnew_ref[b]
        if diag:
            o_ref[b] = jnp.sum(o * diag_ref[...], axis=0, keepdims=True)
        else:
            o_ref[b] = o


def _decode_t(q_blk, k_new, v_new, bias, kt_all, vt_all, layer, bt, sink=None, diag=None):
    bs, nh, W = q_blk.shape
    J = kt_all.shape[3]
    nbr = bias.shape[0]
    in_specs = [pl.BlockSpec((bt, nh, W), lambda i: (i, 0, 0)), pl.BlockSpec((bt, 1, W), lambda i: (i, 0, 0)),
                pl.BlockSpec((bt, 1, W), lambda i: (i, 0, 0)), _resident(bias.shape)]
    args = [q_blk, k_new.reshape(bs, 1, W), v_new.reshape(bs, 1, W), bias]
    for extra in (sink, diag):
        if extra is not None:
            in_specs.append(_resident(extra.shape))
            args.append(extra)
    cache_spec = pl.BlockSpec((1, bt, W, J), lambda i: (layer, i, 0, 0))
    in_specs += [cache_spec, cache_spec]
    args += [kt_all, vt_all]
    out_rows = 1 if diag is not None else nh
    return pl.pallas_call(
        functools.partial(_decode_t_body, bt=bt, nbr=nbr, sink=sink is not None, diag=diag is not None,
                          scale=HEAD_DIM ** -0.5),
        grid=(bs // bt,),
        in_specs=in_specs,
        out_specs=pl.BlockSpec((bt, out_rows, W), lambda i: (i, 0, 0)),
        out_shape=jax.ShapeDtypeStruct((bs, out_rows, W), F32),
        compiler_params=_cparams("parallel"),
        name="decode_t",
    )(*args)


def _decode_mem_body(q_ref, k_ref, v_ref, o_ref, *, bt, scale):
    for b in range(bt):
        qb = q_ref[b] * scale
        s = jnp.sum(k_ref[0, b] * qb[None], axis=2, keepdims=True)
        m = jnp.max(s, axis=0, keepdims=True)
        p = jnp.exp(s - m)
        den = jnp.sum(p, axis=0, keepdims=True)
        o_ref[b] = jnp.sum((p / den) * v_ref[0, b], axis=0)


def _decode_mem(q, k_all, v_all, layer, bt):
    bs, nh, dh = q.shape
    mlen = k_all.shape[2]
    cache_spec = pl.BlockSpec((1, bt, mlen, nh, dh), lambda i: (layer, i, 0, 0, 0))
    return pl.pallas_call(
        functools.partial(_decode_mem_body, bt=bt, scale=dh ** -0.5),
        grid=(bs // bt,),
        in_specs=[pl.BlockSpec((bt, nh, dh), lambda i: (i, 0, 0)), cache_spec, cache_spec],
        out_specs=pl.BlockSpec((bt, nh, dh), lambda i: (i, 0, 0)),
        out_shape=jax.ShapeDtypeStruct((bs, nh, dh), F32),
        compiler_params=_cparams("parallel"),
        name="decode_mem",
    )(q, k_all, v_all)


def _conv_sample_body(p_ref, b0_ref, b1_ref, cw_ref, ya_ref, u_ref, *, width):
    bg = p_ref[:, 0:width]
    u = p_ref[:, width:2 * width] * p_ref[:, 2 * width:3 * width]
    conv = cw_ref[0:1, :] * b0_ref[...] + cw_ref[1:2, :] * b1_ref[...] + cw_ref[2:3, :] * u
    ya_ref[...] = (bg * conv).astype(BF16)
    u_ref[...] = u


def _conv_sample(proj, buf0, buf1, cw):
    bs = proj.shape[0]
    width = cw.shape[1]
    full = lambda shape: pl.BlockSpec(shape, lambda i: (0,) * len(shape))
    return pl.pallas_call(
        functools.partial(_conv_sample_body, width=width),
        grid=(1,),
        in_specs=[full((bs, 3 * width)), full((bs, width)), full((bs, width)), full(cw.shape)],
        out_specs=[full((bs, width)), full((bs, width))],
        out_shape=[jax.ShapeDtypeStruct((bs, width), BF16), jax.ShapeDtypeStruct((bs, width), F32)],
        compiler_params=_cparams("arbitrary"),
        name="conv_sample",
    )(proj, buf0, buf1, cw)


def _pool_sample_body(p_ref, buf_ref, pw_ref, sc_ref, y_ref):
    u = p_ref[...]
    outs = []
    for g, w in enumerate(POOL_WINDOWS):
        sl = slice(g * POOL_GC, (g + 1) * POOL_GC)
        ug = u[:, sl]
        acc = ug
        for i in range(1, w):
            acc = acc + buf_ref[POOL_PAD - i][:, sl]
        cnt = float(min(PAST_LEN + 1, w))
        dlt = (acc / cnt - ug).astype(BF16)
        outs.append(_dot(dlt, pw_ref[g]))
    y_ref[...] = (jnp.concatenate(outs, axis=1) * sc_ref[...]).astype(BF16)


def _pool_sample(proj, buf_t, pw, scale):
    bs = proj.shape[0]
    width = scale.shape[0]
    full = lambda shape: pl.BlockSpec(shape, lambda i: (0,) * len(shape))
    return pl.pallas_call(
        _pool_sample_body,
        grid=(1,),
        in_specs=[full((bs, width)), full(buf_t.shape), full(pw.shape), full((1, width))],
        out_specs=full((bs, width)),
        out_shape=jax.ShapeDtypeStruct((bs, width), BF16),
        compiler_params=_cparams("arbitrary"),
        name="pool_sample",
    )(proj, buf_t, pw, scale.reshape(1, width))


def kernel(x_prompt, x_sample, mem_prompt, cache_conv, cache_swa_k, cache_swa_v, state_pool, cache_dil_k, cache_dil_v, cache_mem_k, cache_mem_v, norm_mix, norm_ca, norm_mem, norm_ffn, norm_final, w_in_even, conv_w, sinks, w_out_even, w_in_odd, pool_w, pool_scale, w_out_odd, w_ca_q, w_ca_kv, w_ca_o, w_ffn_in, w_ffn_out):
    B, L, D = x_prompt.shape
    Bs = x_sample.shape[0]
    depth = norm_mix.shape[0]
    mlen = mem_prompt.shape[1]
    A = conv_w.shape[2]
    HQ = D // 2
    n_q = HQ // HEAD_DIM
    n_kv = cache_swa_k.shape[3]
    rep = n_q // n_kv
    KV = n_kv * HEAD_DIM
    dh_ca = D // CA_HEADS
    TM = 512
    TL = 512

    w_in_e, w_out_e, w_in_o, w_out_o = (w.astype(BF16) for w in (w_in_even, w_out_even, w_in_odd, w_out_odd))
    w_q, w_kv, w_o, w_f_in, w_f_out = (w.astype(BF16) for w in (w_ca_q, w_ca_kv, w_ca_o, w_ffn_in, w_ffn_out))
    pool_w_b = pool_w.astype(BF16)

    xp = x_prompt.reshape(B * L, D)
    xs = x_sample.reshape(Bs, D)
    mem = mem_prompt.reshape(B * mlen, D)

    swa_len, dil_len = cache_swa_k.shape[2], cache_dil_k.shape[2]
    to_t = lambda c: jnp.transpose(c, (0, 1, 3, 4, 2)).reshape(c.shape[0], Bs, c.shape[3] * HEAD_DIM, c.shape[2])
    swa_kt, swa_vt, dil_kt, dil_vt = to_t(cache_swa_k), to_t(cache_swa_v), to_t(cache_dil_k), to_t(cache_dil_v)

    pos = jnp.arange(dil_len)
    dil_bias = jnp.stack([jnp.where((pos >= dil_len - w) & ((dil_len - pos) % d == 0), 0.0, NEG_INF)
                          for w, d in DIL_PATTERNS]).astype(F32).reshape(len(DIL_PATTERNS), 1, dil_len)
    swa_bias = jnp.zeros((1, 1, swa_len), F32)
    head_of_row = jnp.arange(n_q)
    dil_diag = jnp.repeat(jnp.eye(n_q, dtype=F32), HEAD_DIM, axis=1)
    swa_diag = jnp.repeat(jax.nn.one_hot(head_of_row // rep, n_kv, dtype=F32), HEAD_DIM, axis=1)

    conv_p, swak_p, swav_p, pool_p, dilk_p, dilv_p, memk_p, memv_p = [], [], [], [], [], [], [], []
    conv_s, swak_s, swav_s, pool_s, dilk_s, dilv_s = [], [], [], [], [], []

    for l in range(depth):
        if l % 2 == 0:
            e = l // 2
            q0, k0, v0 = 3 * A, 3 * A + HQ, 3 * A + HQ + KV
            proj = _norm_matmul(xp, norm_mix[l], w_in_e, e, TM).reshape(B, L, -1)
            ya, tail = _conv_prompt(proj, conv_w[e], TL)
            sink_row = jnp.repeat(sinks[e], HEAD_DIM).reshape(1, HQ)
            yb = _band_attn(proj, q0 // LANES, k0 // LANES, v0 // LANES, ((BLK, 1),), True, sink_row)
            xp = _matmul_residual(xp, [ya.reshape(B * L, A), yb.reshape(B * L, HQ)], w_out_e, e, TM)
            conv_p.append(tail[:, SUBLANES - (CONV_W - 1):])
            swak_p.append(proj[:, L - BLK:, k0:k0 + KV].reshape(B, BLK, n_kv, HEAD_DIM))
            swav_p.append(proj[:, L - BLK:, v0:v0 + KV].reshape(B, BLK, n_kv, HEAD_DIM))
            proj_s = _norm_matmul(xs, norm_mix[l], w_in_e, e, Bs)
            ya_s, u_s = _conv_sample(proj_s, cache_conv[e][:, 0], cache_conv[e][:, 1], conv_w[e])
            conv_s.append(jnp.stack([cache_conv[e][:, 1], u_s], axis=1))
            k_new, v_new = proj_s[:, k0:k0 + KV], proj_s[:, v0:v0 + KV]
            q_h = proj_s[:, q0:q0 + HQ].reshape(Bs, n_q, 1, HEAD_DIM)
            q_blk = (q_h * swa_diag.reshape(n_q, n_kv, HEAD_DIM)[None]).reshape(Bs, n_q, KV)
            sink_col = jnp.broadcast_to(sinks[e][:, None], (n_q, LANES))
            o_raw = _decode_t(q_blk, k_new, v_new, swa_bias, swa_kt, swa_vt, e, 8, sink=sink_col)
            o_raw = o_raw.reshape(Bs, n_kv, rep, n_kv, HEAD_DIM)
            yb_s = jnp.stack([o_raw[:, g, :, g, :] for g in range(n_kv)], axis=1).reshape(Bs, HQ)
            xs = _matmul_residual(xs, [ya_s, yb_s], w_out_e, e, Bs)
            swak_s.append(k_new.reshape(Bs, 1, n_kv, HEAD_DIM))
            swav_s.append(v_new.reshape(Bs, 1, n_kv, HEAD_DIM))
        else:
            o = l // 2
            q0, k0, v0 = A, A + HQ, A + 2 * HQ
            proj, kt, vt = _norm_matmul_kvt(xp.reshape(B, L, D), norm_mix[l], w_in_o, o, TM, k0, v0, HQ)
            yc = _pool_prompt(proj, pool_w_b[o], pool_scale[o], TL)
            yd = _band_attn(proj, q0 // LANES, k0 // LANES, v0 // LANES, DIL_PATTERNS, False)
            xp = _matmul_residual(xp, [yc.reshape(B * L, A), yd.reshape(B * L, HQ)], w_out_o, o, TM)
            pool_p.append(proj[:, L - POOL_PAD:, :A])
            dilk_p.append(kt)
            dilv_p.append(vt)
            proj_s = _norm_matmul(xs, norm_mix[l], w_in_o, o, Bs)
            yc_s = _pool_sample(proj_s, state_pool[o].transpose(1, 0, 2), pool_w_b[o], pool_scale[o])
            pool_s.append(jnp.concatenate([state_pool[o][:, 1:], proj_s[:, None, :A]], axis=1))
            k_new, v_new = proj_s[:, k0:k0 + HQ], proj_s[:, v0:v0 + HQ]
            q_blk = proj_s[:, None, q0:q0 + HQ] * dil_diag[None]
            yd_s = _decode_t(q_blk, k_new, v_new, dil_bias, dil_kt, dil_vt, o, 1, diag=dil_diag).reshape(Bs, HQ)
            xs = _matmul_residual(xs, [yc_s, yd_s], w_out_o, o, Bs)
            dilk_s.append(k_new.reshape(Bs, 1, n_q, HEAD_DIM))
            dilv_s.append(v_new.reshape(Bs, 1, n_q, HEAD_DIM))

        mkv = _norm_matmul(mem, norm_mem[l], w_kv, l, TM).reshape(B, mlen, 2 * D)
        memk_p.append(mkv[:, :, :D].reshape(B, mlen, CA_HEADS, dh_ca))
        memv_p.append(mkv[:, :, D:].reshape(B, mlen, CA_HEADS, dh_ca))
        xp = _cross_prompt(xp.reshape(B, L, D), norm_ca[l], w_q, mkv, w_o, l, TL).reshape(B * L, D)
        q_s = _norm_matmul(xs, norm_ca[l], w_q, l, Bs).reshape(Bs, CA_HEADS, dh_ca)
        o_s = _decode_mem(q_s, cache_mem_k, cache_mem_v, l, 8).reshape(Bs, D)
        xs = _matmul_residual(xs, [o_s], w_o, l, Bs)

        g_final = norm_final if l == depth - 1 else None
        xp = _swiglu(xp, norm_ffn[l], w_f_in, w_f_out, l, TM, g_final)
        xs = _swiglu(xs, norm_ffn[l], w_f_in, w_f_out, l, Bs, g_final)

    from_t = lambda ts: jnp.transpose(jnp.stack(ts).reshape(len(ts), B, n_q, HEAD_DIM, L), (0, 1, 4, 2, 3))
    return (xp.reshape(B, L, D), xs.reshape(Bs, 1, D),
            jnp.stack(conv_p), jnp.stack(swak_p), jnp.stack(swav_p),
            jnp.stack(pool_p), from_t(dilk_p), from_t(dilv_p),
            jnp.stack(memk_p), jnp.stack(memv_p),
            jnp.stack(conv_s), jnp.stack(swak_s), jnp.stack(swav_s),
            jnp.stack(pool_s), jnp.stack(dilk_s), jnp.stack(dilv_s))
```

```python
import functools

import jax
import jax.numpy as jnp
from jax import lax
from jax.experimental import pallas as pl
from jax.experimental.pallas import tpu as pltpu

F32 = jnp.float32
BF16 = jnp.bfloat16

LANES = 128
SUBLANES = 8
VMEM_LIMIT = 56 * 1024 * 1024

HEAD_DIM = 64
BLK = 128
CONV_W = 3
POOL_WINDOWS = (2, 4, 8, 16)
POOL_PAD = 15
POOL_GC = 128
DIL_PATTERNS = ((128, 1), (512, 4), (2048, 16))
CA_HEADS = 4
PAST_LEN = 8192
RMS_EPS = 1e-6
NEG_INF = -1e30


def _cparams(*sem):
    return pltpu.CompilerParams(dimension_semantics=sem, vmem_limit_bytes=VMEM_LIMIT)


def _rms(x, g):
    ms = jnp.mean(x * x, axis=-1, keepdims=True)
    return x * lax.rsqrt(ms + RMS_EPS) * g


def _dot(a, b):
    return jnp.dot(a, b, preferred_element_type=F32)


def _dot_nt(a, b):
    return lax.dot_general(a, b, (((1,), (1,)), ((), ())), preferred_element_type=F32)


def _resident(shape):
    return pl.BlockSpec(shape, lambda *_: (0,) * len(shape), pipeline_mode=pl.Buffered(1))


def _layer_block(stack, layer, rows=None, row_block=0):
    _, k, n = stack.shape
    return pl.BlockSpec((None, k if rows is None else rows, n), lambda *_: (layer, row_block, 0),
                        pipeline_mode=pl.Buffered(1))


def _norm_matmul_body(x_ref, g_ref, w_ref, o_ref):
    h = _rms(x_ref[...], g_ref[...]).astype(BF16)
    o_ref[...] = _dot(h, w_ref[...])


def _norm_matmul(x, g, w, layer, tm):
    m, d = x.shape
    n = w.shape[2]
    return pl.pallas_call(
        _norm_matmul_body,
        grid=(m // tm,),
        in_specs=[pl.BlockSpec((tm, d), lambda i: (i, 0)), _resident((1, d)), _layer_block(w, layer)],
        out_specs=pl.BlockSpec((tm, n), lambda i: (i, 0)),
        out_shape=jax.ShapeDtypeStruct((m, n), F32),
        compiler_params=_cparams("parallel"),
        name="norm_matmul",
    )(x, g.reshape(1, d), w)


def _norm_matmul_kvt_body(x_ref, g_ref, w_ref, o_ref, kt_ref, vt_ref, *, k0, v0, width):
    h = _rms(x_ref[0], g_ref[...]).astype(BF16)
    r = _dot(h, w_ref[...])
    o_ref[0] = r
    kt_ref[0] = r[:, k0:k0 + width].T
    vt_ref[0] = r[:, v0:v0 + width].T


def _norm_matmul_kvt(x, g, w, layer, tm, k0, v0, width):
    b, l, d = x.shape
    n = w.shape[2]
    t_spec = pl.BlockSpec((1, width, tm), lambda i, j: (i, 0, j))
    return pl.pallas_call(
        functools.partial(_norm_matmul_kvt_body, k0=k0, v0=v0, width=width),
        grid=(b, l // tm),
        in_specs=[pl.BlockSpec((1, tm, d), lambda i, j: (i, j, 0)), _resident((1, d)), _layer_block(w, layer)],
        out_specs=[pl.BlockSpec((1, tm, n), lambda i, j: (i, j, 0)), t_spec, t_spec],
        out_shape=[jax.ShapeDtypeStruct((b, l, n), F32), jax.ShapeDtypeStruct((b, width, l), F32),
                   jax.ShapeDtypeStruct((b, width, l), F32)],
        compiler_params=_cparams("parallel", "parallel"),
        name="norm_matmul_kvt",
    )(x, g.reshape(1, d), w)


def _matmul_residual_body(*refs, nparts):
    x_ref, o_ref = refs[0], refs[-1]
    acc = x_ref[...]
    for i in range(nparts):
        acc = acc + _dot(refs[1 + 2 * i][...].astype(BF16), refs[2 + 2 * i][...])
    o_ref[...] = acc


def _matmul_residual(x, parts, w, layer, tm):
    m, d = x.shape
    in_specs = [pl.BlockSpec((tm, d), lambda i: (i, 0))]
    args = [x]
    for row_block, a in enumerate(parts):
        in_specs += [pl.BlockSpec((tm, a.shape[1]), lambda i: (i, 0)), _layer_block(w, layer, a.shape[1], row_block)]
        args += [a, w]
    return pl.pallas_call(
        functools.partial(_matmul_residual_body, nparts=len(parts)),
        grid=(m // tm,),
        in_specs=in_specs,
        out_specs=pl.BlockSpec((tm, d), lambda i: (i, 0)),
        out_shape=jax.ShapeDtypeStruct((m, d), F32),
        compiler_params=_cparams("parallel"),
        name="matmul_residual",
    )(*args)


FFN_CHUNK = 256


def _swiglu_body(*refs, hidden, final, nsub, host):
    it = iter(refs)
    x_ref, g_ref, win_ref, wout_ref = next(it), next(it), next(it), next(it)
    gf_ref = next(it) if final else None
    if host is not None:
        host_in = [next(it) for _ in range(host[2])]
    o_ref = next(it)
    host_out = next(it) if host is not None else None
    h_scr, acc_scr = next(it), next(it)
    j = pl.program_id(1)

    @pl.when(j == 0)
    def _():
        x = x_ref[...]
        h_scr[...] = _rms(x, g_ref[...]).astype(BF16)
        acc_scr[...] = x

    nchunk = hidden // FFN_CHUNK
    per = -(-nchunk // nsub)
    for grp in range(nsub):
        @pl.when(j == grp)
        def _(grp=grp):
            h = h_scr[...]
            acc = acc_scr[...]
            first, last = grp * per, min((grp + 1) * per, nchunk)
            begin_at = first + 1 if last - first > 1 else first
            for c in range(first, last):
                if host is not None and c == begin_at:
                    state = host[0](*host_in)
                lo = c * FFN_CHUNK
                gate = _dot(h, win_ref[:, lo:lo + FFN_CHUNK])
                up = _dot(h, win_ref[:, hidden + lo:hidden + lo + FFN_CHUNK])
                act = (gate * (1.0 / (1.0 + jnp.exp(-gate))) * up).astype(BF16)
                acc = acc + _dot(act, wout_ref[lo:lo + FFN_CHUNK, :])
            if host is not None:
                host[1](state, *host_in, host_out)
            if grp < nsub - 1:
                acc_scr[...] = acc
            else:
                o_ref[...] = _rms(acc, gf_ref[...]) if final else acc


def _swiglu(x, g, win, wout, layer, tm, g_final=None, host=None):
    m, d = x.shape
    hidden = wout.shape[1]
    final = g_final is not None
    steps = m // tm
    in_specs = [pl.BlockSpec((tm, d), lambda i, j: (i, 0)), _resident((1, d)), _layer_block(win, layer),
                _layer_block(wout, layer)]
    args = [x, g.reshape(1, d), win, wout]
    if final:
        in_specs.append(_resident((1, d)))
        args.append(g_final.reshape(1, d))
    out_specs = [pl.BlockSpec((tm, d), lambda i, j: (i, 0))]
    out_shape = [jax.ShapeDtypeStruct((m, d), F32)]
    nsub, host_arg = 1, None
    if host is not None:
        begin_fn, end_fn, specs_fn, bs = host
        nsub = bs // steps
        h_specs, h_args, h_out_spec, h_out_shape = specs_fn(lambda i, j: i * nsub + j)
        in_specs += h_specs
        args += h_args
        out_specs.append(h_out_spec)
        out_shape.append(h_out_shape)
        host_arg = (begin_fn, end_fn, len(h_specs))
    res = pl.pallas_call(
        functools.partial(_swiglu_body, hidden=hidden, final=final, nsub=nsub, host=host_arg),
        grid=(steps, nsub),
        in_specs=in_specs,
        out_specs=out_specs,
        out_shape=out_shape,
        scratch_shapes=[pltpu.VMEM((tm, d), BF16), pltpu.VMEM((tm, d), F32)],
        compiler_params=_cparams("parallel", "arbitrary"),
        name="swiglu",
    )(*args)
    return res if host is not None else res[0]


def _conv_prompt_body(p_ref, cw_ref, ya_ref, tail_ref, ubuf, *, tl, width):
    @pl.when(pl.program_id(1) == 0)
    def _():
        ubuf[0:SUBLANES, :] = jnp.zeros((SUBLANES, width), F32)

    bg = p_ref[0, :, 0:width]
    u = p_ref[0, :, width:2 * width] * p_ref[0, :, 2 * width:3 * width]
    ubuf[SUBLANES:SUBLANES + tl, :] = u
    conv = (cw_ref[0:1, :] * ubuf[SUBLANES - 2:SUBLANES - 2 + tl, :]
            + cw_ref[1:2, :] * ubuf[SUBLANES - 1:SUBLANES - 1 + tl, :]
            + cw_ref[2:3, :] * u)
    ya_ref[0] = (bg * conv).astype(BF16)
    tail = ubuf[tl:tl + SUBLANES, :]
    tail_ref[0] = tail
    ubuf[0:SUBLANES, :] = tail


def _conv_prompt(proj, cw, tl):
    b, l, _ = proj.shape
    width = cw.shape[1]
    return pl.pallas_call(
        functools.partial(_conv_prompt_body, tl=tl, width=width),
        grid=(b, l // tl),
        in_specs=[pl.BlockSpec((1, tl, 3 * width), lambda i, j: (i, j, 0)), _resident(cw.shape)],
        out_specs=[pl.BlockSpec((1, tl, width), lambda i, j: (i, j, 0)),
                   pl.BlockSpec((1, SUBLANES, width), lambda i, j: (i, 0, 0))],
        out_shape=[jax.ShapeDtypeStruct((b, l, width), BF16), jax.ShapeDtypeStruct((b, SUBLANES, width), F32)],
        scratch_shapes=[pltpu.VMEM((tl + SUBLANES, width), F32)],
        compiler_params=_cparams("parallel", "arbitrary"),
        name="conv_prompt",
    )(proj, cw)


POOL_CARRY = 16


def _pool_prompt_body(p_ref, pw_ref, sc_ref, y_ref, ubuf, *, tl, width):
    @pl.when(pl.program_id(1) == 0)
    def _():
        ubuf[0:POOL_CARRY, :] = jnp.zeros((POOL_CARRY, width), F32)

    u = p_ref[0]
    ubuf[POOL_CARRY:POOL_CARRY + tl, :] = u
    pos = pl.program_id(1) * tl + lax.broadcasted_iota(jnp.int32, (tl, 1), 0)
    outs = []
    for g, w in enumerate(POOL_WINDOWS):
        sl = slice(g * POOL_GC, (g + 1) * POOL_GC)
        ug = u[:, sl]
        acc = ug
        for i in range(1, w):
            acc = acc + ubuf[POOL_CARRY - i:POOL_CARRY - i + tl, sl]
        cnt = jnp.minimum(pos + 1, w).astype(F32)
        dlt = (acc / cnt - ug).astype(BF16)
        outs.append(_dot(dlt, pw_ref[g]))
    y_ref[0] = (jnp.concatenate(outs, axis=1) * sc_ref[...]).astype(BF16)
    ubuf[0:POOL_CARRY, :] = ubuf[tl:tl + POOL_CARRY, :]


def _pool_prompt(proj, pw, scale, tl):
    b, l, _ = proj.shape
    width = scale.shape[0]
    return pl.pallas_call(
        functools.partial(_pool_prompt_body, tl=tl, width=width),
        grid=(b, l // tl),
        in_specs=[pl.BlockSpec((1, tl, width), lambda i, j: (i, j, 0)), _resident(pw.shape), _resident((1, width))],
        out_specs=pl.BlockSpec((1, tl, width), lambda i, j: (i, j, 0)),
        out_shape=jax.ShapeDtypeStruct((b, l, width), BF16),
        scratch_shapes=[pltpu.VMEM((tl + POOL_CARRY, width), F32)],
        compiler_params=_cparams("parallel", "arbitrary"),
        name="pool_prompt",
    )(proj, pw, scale.reshape(1, width))


TILES_IN_FLIGHT = 5


def _unroll(trips, limit=TILES_IN_FLIGHT):
    return max(u for u in range(1, max(limit, 1) + 1) if trips % u == 0)


def _band_attn_body(*refs, branches, gqa, sink, seq):
    nbr = len(branches)
    if sink:
        q_ref, k_ref, v_ref, sink_ref, y_ref = refs[:5]
        scr = refs[5:]
    else:
        q_ref, k_ref, v_ref, y_ref = refs[:4]
        scr = refs[4:]
    o_scr, l_scr = scr[:nbr], scr[nbr:2 * nbr]
    bias_first, bias_band = scr[2 * nbr], scr[2 * nbr + 1]
    lane = lax.broadcasted_iota(jnp.int32, (1, LANES), 1)
    lo = lane < HEAD_DIM

    rr = lax.broadcasted_iota(jnp.int32, (2 * BLK, 2 * BLK), 0) & (BLK - 1)
    cc = lax.broadcasted_iota(jnp.int32, (2 * BLK, 2 * BLK), 1)
    bias_band[...] = jnp.where((cc >= rr) & (cc <= rr + BLK), 0.0, NEG_INF).astype(F32)
    rr1 = lax.broadcasted_iota(jnp.int32, (2 * BLK, BLK), 0) & (BLK - 1)
    cc1 = lax.broadcasted_iota(jnp.int32, (2 * BLK, BLK), 1)
    bias_first[...] = jnp.where(cc1 <= rr1, 0.0, NEG_INF).astype(F32)

    if gqa:
        kx, vx = scr[2 * nbr + 2], scr[2 * nbr + 3]
        keep = (lane // HEAD_DIM) == (pl.program_id(1) // 2)
        chunk = 2 * BLK
        for c in range(seq // chunk):
            rows = slice(c * chunk, (c + 1) * chunk)
            kb = k_ref[0, rows, :]
            vb = v_ref[0, rows, :]
            kx[rows, :] = jnp.where(keep, kb, pltpu.roll(kb, HEAD_DIM, axis=1))
            vx[rows, :] = jnp.where(keep, vb, pltpu.roll(vb, HEAD_DIM, axis=1))
        load_k = lambda idx: kx[idx, :]
        load_v = lambda idx: vx[idx, :]
    else:
        load_k = lambda idx: k_ref[0, idx, :]
        load_v = lambda idx: v_ref[0, idx, :]
    load_q = lambda idx: q_ref[0, idx, :]

    scale = HEAD_DIM ** -0.5

    def tile(q_t, k_t, v_t, bias):
        qs = q_t * scale
        qm = jnp.concatenate([jnp.where(lo, qs, 0.0), jnp.where(lo, 0.0, qs)], axis=0).astype(BF16)
        s = _dot_nt(qm, k_t.astype(BF16)) + bias
        m = jnp.max(s, axis=1, keepdims=True)
        p = jnp.exp(s - m)
        den = jnp.sum(p, axis=1, keepdims=True)
        o2 = _dot(p.astype(BF16), v_t.astype(BF16)) / den
        lse = jnp.broadcast_to(m + jnp.log(den), (2 * BLK, LANES))
        o = jnp.where(lo, o2[:BLK], o2[BLK:])
        return o, jnp.where(lo, lse[:BLK], lse[BLK:])

    for bi, (w, d) in enumerate(branches):
        nblk = seq // (d * BLK)

        def rows(start, size, d=d):
            if isinstance(start, int):
                return pl.ds(start, size) if d == 1 else pl.ds(start, size, stride=d)
            if d == 1:
                return pl.ds(pl.multiple_of(start, BLK), size)
            return pl.ds(start, size, stride=d)

        def run_class(r, bi=bi, d=d, nblk=nblk, rows=rows):
            idx0 = rows(r, BLK)
            o, lse = tile(load_q(idx0), load_k(idx0), load_v(idx0), bias_first[...])
            o_scr[bi][idx0, :] = o
            l_scr[bi][idx0, :] = lse

            def body(i, carry):
                qi = rows(r + d * BLK * i, BLK)
                ki = rows(r + d * BLK * (i - 1), 2 * BLK)
                o, lse = tile(load_q(qi), load_k(ki), load_v(ki), bias_band[...])
                o_scr[bi][qi, :] = o
                l_scr[bi][qi, :] = lse
                return carry

            if nblk > 1:
                lax.fori_loop(1, nblk, body, 0, unroll=_unroll(nblk - 1))

        if d == 1:
            run_class(0)
        else:
            lax.fori_loop(0, d, lambda r, c, run_class=run_class: (run_class(r), c)[1], 0,
                          unroll=_unroll(d, TILES_IN_FLIGHT // min(nblk, TILES_IN_FLIGHT)))

    chunk = 2 * BLK
    for c in range(seq // chunk):
        rows_c = slice(c * chunk, (c + 1) * chunk)
        if sink:
            lse = l_scr[0][rows_c, :]
            gate = 1.0 / (1.0 + jnp.exp(sink_ref[...] - lse))
            y = o_scr[0][rows_c, :] * gate
        else:
            lses = [l_scr[bi][rows_c, :] for bi in range(nbr)]
            top = functools.reduce(jnp.maximum, lses)
            es = [jnp.exp(x - top) for x in lses]
            y = sum(e * o_scr[bi][rows_c, :] for bi, e in enumerate(es)) / sum(es)
        y_ref[0, rows_c, :] = y.astype(BF16)


def _band_attn(proj, q_blk, k_blk, v_blk, branches, gqa, sink_row=None):
    b, seq, _ = proj.shape
    nlb = 4
    nbr = len(branches)
    sink = sink_row is not None
    blk = (1, seq, LANES)
    kv_map = (lambda off: (lambda i, j: (i, 0, off))) if gqa else (lambda off: (lambda i, j: (i, 0, off + j)))
    in_specs = [pl.BlockSpec(blk, lambda i, j: (i, 0, q_blk + j)), pl.BlockSpec(blk, kv_map(k_blk)),
                pl.BlockSpec(blk, kv_map(v_blk))]
    args = [proj, proj, proj]
    if sink:
        in_specs.append(pl.BlockSpec((1, LANES), lambda i, j: (0, j)))
        args.append(sink_row)
    scratch = [pltpu.VMEM((seq, LANES), F32) for _ in range(2 * nbr)]
    scratch += [pltpu.VMEM((2 * BLK, BLK), F32), pltpu.VMEM((2 * BLK, 2 * BLK), F32)]
    if gqa:
        scratch += [pltpu.VMEM((seq, LANES), F32), pltpu.VMEM((seq, LANES), F32)]
    return pl.pallas_call(
        functools.partial(_band_attn_body, branches=branches, gqa=gqa, sink=sink, seq=seq),
        grid=(b, nlb),
        in_specs=in_specs,
        out_specs=pl.BlockSpec(blk, lambda i, j: (i, 0, j)),
        out_shape=jax.ShapeDtypeStruct((b, seq, nlb * LANES), BF16),
        scratch_shapes=scratch,
        compiler_params=_cparams("parallel", "parallel"),
        name="band_attn",
    )(*args)


def _cross_prompt_body(x_ref, g_ref, wq_ref, mkv_ref, wo_ref, o_ref, *, heads):
    x = x_ref[0]
    d = x.shape[1]
    dh = d // heads
    h = _rms(x, g_ref[...]).astype(BF16)
    q = (_dot(h, wq_ref[...]) * (dh ** -0.5)).astype(BF16)
    outs = []
    for c in range(heads):
        kc = mkv_ref[0, :, c * dh:(c + 1) * dh].astype(BF16)
        vc = mkv_ref[0, :, d + c * dh:d + (c + 1) * dh].astype(BF16)
        s = _dot_nt(q[:, c * dh:(c + 1) * dh], kc)
        m = jnp.max(s, axis=1, keepdims=True)
        p = jnp.exp(s - m)
        den = jnp.sum(p, axis=1, keepdims=True)
        outs.append((_dot(p.astype(BF16), vc) / den).astype(BF16))
    o_ref[0] = x + _dot(jnp.concatenate(outs, axis=1), wo_ref[...])


def _cross_prompt(x, g, wq, mkv, wo, layer, tl):
    b, l, d = x.shape
    mlen = mkv.shape[1]
    return pl.pallas_call(
        functools.partial(_cross_prompt_body, heads=CA_HEADS),
        grid=(b, l // tl),
        in_specs=[pl.BlockSpec((1, tl, d), lambda i, j: (i, j, 0)), _resident((1, d)), _layer_block(wq, layer),
                  pl.BlockSpec((1, mlen, 2 * d), lambda i, j: (i, 0, 0)), _layer_block(wo, layer)],
        out_specs=pl.BlockSpec((1, tl, d), lambda i, j: (i, j, 0)),
        out_shape=jax.ShapeDtypeStruct((b, l, d), F32),
        compiler_params=_cparams("parallel", "parallel"),
        name="cross_prompt",
    )(x, g.reshape(1, d), wq, mkv, wo)


def _decode_t_scores(q_ref, knew_ref, bias_ref, sink_ref, k_ref, b, scale):
    qb = q_ref[b] * scale
    s = _dot(qb.astype(BF16), k_ref[0, b].astype(BF16))
    sn = jnp.sum(qb * knew_ref[b], axis=1, keepdims=True)
    stats = []
    for br in range(bias_ref.shape[0]):
        s_ = s + bias_ref[br]
        m = jnp.maximum(jnp.max(s_, axis=1, keepdims=True), sn)
        p = jnp.exp(s_ - m)
        pn = jnp.exp(sn - m)
        den = jnp.sum(p, axis=1, keepdims=True) + pn
        stats.append((p, pn, den, m + jnp.log(den)))
    if sink_ref is not None:
        facs = [1.0 / ((1.0 + jnp.exp(sink_ref[:, 0:1] - st[3])) * st[2]) for st in stats]
    else:
        top = functools.reduce(jnp.maximum, [st[3] for st in stats])
        es = [jnp.exp(st[3] - top) for st in stats]
        tot = sum(es)
        facs = [e / (tot * st[2]) for e, st in zip(es, stats)]
    p_all = sum(st[0] * f for st, f in zip(stats, facs))
    pn_all = sum(st[1] * f for st, f in zip(stats, facs))
    return p_all.astype(BF16), pn_all


def _decode_t_out(weights, vnew_ref, diag_ref, v_ref, o_ref, b):
    p_all, pn_all = weights
    o = _dot_nt(p_all, v_ref[0, b].astype(BF16)) + pn_all * vnew_ref[b]
    if diag_ref is not None:
        o_ref[b] = jnp.sum(o * diag_ref[...], axis=0, keepdims=True)
    else:
        o_ref[b] = o


def _decode_t_body(*refs, bt, sink, diag, scale):
    it = iter(refs)
    q_ref, knew_ref, vnew_ref, bias_ref = next(it), next(it), next(it), next(it)
    sink_ref = next(it) if sink else None
    diag_ref = next(it) if diag else None
    k_ref, v_ref, o_ref = next(it), next(it), next(it)
    weights = [_decode_t_scores(q_ref, knew_ref, bias_ref, sink_ref, k_ref, b, scale) for b in range(bt)]
    for b in range(bt):
        _decode_t_out(weights[b], vnew_ref, diag_ref, v_ref, o_ref, b)


def _decode_t_specs(q_blk, k_new, v_new, bias, kt_all, vt_all, layer, bt, sink, diag, row_of):
    bs, nh, W = q_blk.shape
    J = kt_all.shape[3]
    in_specs = [pl.BlockSpec((bt, nh, W), lambda *g: (row_of(*g), 0, 0)),
                pl.BlockSpec((bt, 1, W), lambda *g: (row_of(*g), 0, 0)),
                pl.BlockSpec((bt, 1, W), lambda *g: (row_of(*g), 0, 0)), _resident(bias.shape)]
    args = [q_blk, k_new.reshape(bs, 1, W), v_new.reshape(bs, 1, W), bias]
    for extra in (sink, diag):
        if extra is not None:
            in_specs.append(_resident(extra.shape))
            args.append(extra)
    cache_spec = pl.BlockSpec((1, bt, W, J), lambda *g: (layer, row_of(*g), 0, 0))
    in_specs += [cache_spec, cache_spec]
    args += [kt_all, vt_all]
    out_rows = 1 if diag is not None else nh
    out_spec = pl.BlockSpec((bt, out_rows, W), lambda *g: (row_of(*g), 0, 0))
    return in_specs, args, out_spec, jax.ShapeDtypeStruct((bs, out_rows, W), F32)


def _decode_t(q_blk, k_new, v_new, bias, kt_all, vt_all, layer, bt, sink=None, diag=None):
    in_specs, args, out_spec, out_shape = _decode_t_specs(q_blk, k_new, v_new, bias, kt_all, vt_all, layer, bt,
                                                          sink, diag, lambda i: i)
    return pl.pallas_call(
        functools.partial(_decode_t_body, bt=bt, sink=sink is not None, diag=diag is not None,
                          scale=HEAD_DIM ** -0.5),
        grid=(q_blk.shape[0] // bt,),
        in_specs=in_specs,
        out_specs=out_spec,
        out_shape=out_shape,
        compiler_params=_cparams("parallel"),
        name="decode_t",
    )(*args)


def _decode_mem_scores(q_ref, k_ref, b, scale):
    qb = q_ref[b] * scale
    s = jnp.sum(k_ref[0, b] * qb[None], axis=2, keepdims=True)
    m = jnp.max(s, axis=0, keepdims=True)
    p = jnp.exp(s - m)
    return p / jnp.sum(p, axis=0, keepdims=True)


def _decode_mem_out(weights, v_ref, o_ref, b):
    o_ref[b] = jnp.sum(weights * v_ref[0, b], axis=0)


def _decode_mem_body(q_ref, k_ref, v_ref, o_ref, *, bt, scale):
    for b in range(bt):
        _decode_mem_out(_decode_mem_scores(q_ref, k_ref, b, scale), v_ref, o_ref, b)


def _decode_mem_specs(q, k_all, v_all, layer, bt, row_of):
    bs, nh, dh = q.shape
    mlen = k_all.shape[2]
    cache_spec = pl.BlockSpec((1, bt, mlen, nh, dh), lambda *g: (layer, row_of(*g), 0, 0, 0))
    q_spec = pl.BlockSpec((bt, nh, dh), lambda *g: (row_of(*g), 0, 0))
    return [q_spec, cache_spec, cache_spec], [q, k_all, v_all], q_spec, jax.ShapeDtypeStruct((bs, nh, dh), F32)


def _decode_mem(q, k_all, v_all, layer, bt):
    in_specs, args, out_spec, out_shape = _decode_mem_specs(q, k_all, v_all, layer, bt, lambda i: i)
    return pl.pallas_call(
        functools.partial(_decode_mem_body, bt=bt, scale=q.shape[2] ** -0.5),
        grid=(q.shape[0] // bt,),
        in_specs=in_specs,
        out_specs=out_spec,
        out_shape=out_shape,
        compiler_params=_cparams("parallel"),
        name="decode_mem",
    )(*args)


def _conv_sample_body(p_ref, b0_ref, b1_ref, cw_ref, ya_ref, u_ref, *, width):
    bg = p_ref[:, 0:width]
    u = p_ref[:, width:2 * width] * p_ref[:, 2 * width:3 * width]
    conv = cw_ref[0:1, :] * b0_ref[...] + cw_ref[1:2, :] * b1_ref[...] + cw_ref[2:3, :] * u
    ya_ref[...] = (bg * conv).astype(BF16)
    u_ref[...] = u


def _conv_sample(proj, buf0, buf1, cw):
    bs = proj.shape[0]
    width = cw.shape[1]
    full = lambda shape: pl.BlockSpec(shape, lambda i: (0,) * len(shape))
    return pl.pallas_call(
        functools.partial(_conv_sample_body, width=width),
        grid=(1,),
        in_specs=[full((bs, 3 * width)), full((bs, width)), full((bs, width)), full(cw.shape)],
        out_specs=[full((bs, width)), full((bs, width))],
        out_shape=[jax.ShapeDtypeStruct((bs, width), BF16), jax.ShapeDtypeStruct((bs, width), F32)],
        compiler_params=_cparams("arbitrary"),
        name="conv_sample",
    )(proj, buf0, buf1, cw)


def _pool_sample_body(p_ref, buf_ref, pw_ref, sc_ref, y_ref):
    u = p_ref[...]
    outs = []
    for g, w in enumerate(POOL_WINDOWS):
        sl = slice(g * POOL_GC, (g + 1) * POOL_GC)
        ug = u[:, sl]
        acc = ug
        for i in range(1, w):
            acc = acc + buf_ref[POOL_PAD - i][:, sl]
        cnt = float(min(PAST_LEN + 1, w))
        dlt = (acc / cnt - ug).astype(BF16)
        outs.append(_dot(dlt, pw_ref[g]))
    y_ref[...] = (jnp.concatenate(outs, axis=1) * sc_ref[...]).astype(BF16)


def _pool_sample(proj, buf_t, pw, scale):
    bs = proj.shape[0]
    width = scale.shape[0]
    full = lambda shape: pl.BlockSpec(shape, lambda i: (0,) * len(shape))
    return pl.pallas_call(
        _pool_sample_body,
        grid=(1,),
        in_specs=[full((bs, width)), full(buf_t.shape), full(pw.shape), full((1, width))],
        out_specs=full((bs, width)),
        out_shape=jax.ShapeDtypeStruct((bs, width), BF16),
        compiler_params=_cparams("arbitrary"),
        name="pool_sample",
    )(proj, buf_t, pw, scale.reshape(1, width))


def kernel(x_prompt, x_sample, mem_prompt, cache_conv, cache_swa_k, cache_swa_v, state_pool, cache_dil_k, cache_dil_v, cache_mem_k, cache_mem_v, norm_mix, norm_ca, norm_mem, norm_ffn, norm_final, w_in_even, conv_w, sinks, w_out_even, w_in_odd, pool_w, pool_scale, w_out_odd, w_ca_q, w_ca_kv, w_ca_o, w_ffn_in, w_ffn_out):
    B, L, D = x_prompt.shape
    Bs = x_sample.shape[0]
    depth = norm_mix.shape[0]
    mlen = mem_prompt.shape[1]
    A = conv_w.shape[2]
    HQ = D // 2
    n_q = HQ // HEAD_DIM
    n_kv = cache_swa_k.shape[3]
    rep = n_q // n_kv
    KV = n_kv * HEAD_DIM
    dh_ca = D // CA_HEADS
    TM = 512
    TL = 512

    w_in_e, w_out_e, w_in_o, w_out_o = (w.astype(BF16) for w in (w_in_even, w_out_even, w_in_odd, w_out_odd))
    w_q, w_kv, w_o, w_f_in, w_f_out = (w.astype(BF16) for w in (w_ca_q, w_ca_kv, w_ca_o, w_ffn_in, w_ffn_out))
    pool_w_b = pool_w.astype(BF16)

    xp = x_prompt.reshape(B * L, D)
    xs = x_sample.reshape(Bs, D)
    mem = mem_prompt.reshape(B * mlen, D)

    swa_len, dil_len = cache_swa_k.shape[2], cache_dil_k.shape[2]
    to_t = lambda c: jnp.transpose(c, (0, 1, 3, 4, 2)).reshape(c.shape[0], Bs, c.shape[3] * HEAD_DIM, c.shape[2])
    swa_kt, swa_vt, dil_kt, dil_vt = to_t(cache_swa_k), to_t(cache_swa_v), to_t(cache_dil_k), to_t(cache_dil_v)

    pos = jnp.arange(dil_len)
    dil_bias = jnp.stack([jnp.where((pos >= dil_len - w) & ((dil_len - pos) % d == 0), 0.0, NEG_INF)
                          for w, d in DIL_PATTERNS]).astype(F32).reshape(len(DIL_PATTERNS), 1, dil_len)
    swa_bias = jnp.zeros((1, 1, swa_len), F32)
    head_of_row = jnp.arange(n_q)
    dil_diag = jnp.repeat(jnp.eye(n_q, dtype=F32), HEAD_DIM, axis=1)
    swa_diag = jnp.repeat(jax.nn.one_hot(head_of_row // rep, n_kv, dtype=F32), HEAD_DIM, axis=1)

    conv_p, swak_p, swav_p, pool_p, dilk_p, dilv_p, memk_p, memv_p = [], [], [], [], [], [], [], []
    conv_s, swak_s, swav_s, pool_s, dilk_s, dilv_s = [], [], [], [], [], []
    attn_scale = HEAD_DIM ** -0.5

    def sample_cross_q(xs, l):
        return _norm_matmul(xs, norm_ca[l], w_q, l, Bs).reshape(Bs, CA_HEADS, dh_ca)

    def sample_ffn(xs, o_s, l):
        xs = _matmul_residual(xs, [o_s.reshape(Bs, D)], w_o, l, Bs)
        return _swiglu(xs, norm_ffn[l], w_f_in, w_f_out, l, Bs, norm_final if l == depth - 1 else None)

    def sample_odd_front(xs, l):
        o = l // 2
        q0, k0, v0 = A, A + HQ, A + 2 * HQ
        proj_s = _norm_matmul(xs, norm_mix[l], w_in_o, o, Bs)
        yc_s = _pool_sample(proj_s, state_pool[o].transpose(1, 0, 2), pool_w_b[o], pool_scale[o])
        pool_s.append(jnp.concatenate([state_pool[o][:, 1:], proj_s[:, None, :A]], axis=1))
        k_new, v_new = proj_s[:, k0:k0 + HQ], proj_s[:, v0:v0 + HQ]
        dilk_s.append(k_new.reshape(Bs, 1, n_q, HEAD_DIM))
        dilv_s.append(v_new.reshape(Bs, 1, n_q, HEAD_DIM))
        q_blk = proj_s[:, None, q0:q0 + HQ] * dil_diag[None]
        begin = lambda q, kn, vn, bias, diag, k, v: _decode_t_scores(q, kn, bias, None, k, 0, attn_scale)
        end = lambda weights, q, kn, vn, bias, diag, k, v, out: _decode_t_out(weights, vn, diag, v, out, 0)
        specs_fn = lambda row_of: _decode_t_specs(q_blk, k_new, v_new, dil_bias, dil_kt, dil_vt, o, 1, None,
                                                  dil_diag, row_of)
        return yc_s, (begin, end, specs_fn, Bs)

    for l in range(depth):
        g_final = norm_final if l == depth - 1 else None
        if l % 2 == 0:
            e = l // 2
            q0, k0, v0 = 3 * A, 3 * A + HQ, 3 * A + HQ + KV
            proj = _norm_matmul(xp, norm_mix[l], w_in_e, e, TM).reshape(B, L, -1)
            ya, tail = _conv_prompt(proj, conv_w[e], TL)
            sink_row = jnp.repeat(sinks[e], HEAD_DIM).reshape(1, HQ)
            yb = _band_attn(proj, q0 // LANES, k0 // LANES, v0 // LANES, ((BLK, 1),), True, sink_row)
            xp = _matmul_residual(xp, [ya.reshape(B * L, A), yb.reshape(B * L, HQ)], w_out_e, e, TM)
            conv_p.append(tail[:, SUBLANES - (CONV_W - 1):])
            swak_p.append(proj[:, L - BLK:, k0:k0 + KV].reshape(B, BLK, n_kv, HEAD_DIM))
            swav_p.append(proj[:, L - BLK:, v0:v0 + KV].reshape(B, BLK, n_kv, HEAD_DIM))
            proj_s = _norm_matmul(xs, norm_mix[l], w_in_e, e, Bs)
            ya_s, u_s = _conv_sample(proj_s, cache_conv[e][:, 0], cache_conv[e][:, 1], conv_w[e])
            conv_s.append(jnp.stack([cache_conv[e][:, 1], u_s], axis=1))
            k_new, v_new = proj_s[:, k0:k0 + KV], proj_s[:, v0:v0 + KV]
            q_h = proj_s[:, q0:q0 + HQ].reshape(Bs, n_q, 1, HEAD_DIM)
            q_blk = (q_h * swa_diag.reshape(n_q, n_kv, HEAD_DIM)[None]).reshape(Bs, n_q, KV)
            sink_col = jnp.broadcast_to(sinks[e][:, None], (n_q, LANES))
            o_raw = _decode_t(q_blk, k_new, v_new, swa_bias, swa_kt, swa_vt, e, 8, sink=sink_col)
            o_raw = o_raw.reshape(Bs, n_kv, rep, n_kv, HEAD_DIM)
            yb_s = jnp.stack([o_raw[:, g, :, g, :] for g in range(n_kv)], axis=1).reshape(Bs, HQ)
            xs = _matmul_residual(xs, [ya_s, yb_s], w_out_e, e, Bs)
            swak_s.append(k_new.reshape(Bs, 1, n_kv, HEAD_DIM))
            swav_s.append(v_new.reshape(Bs, 1, n_kv, HEAD_DIM))
            xs = sample_ffn(xs, _decode_mem(sample_cross_q(xs, l), cache_mem_k, cache_mem_v, l, 8), l)
        else:
            o = l // 2
            q0, k0, v0 = A, A + HQ, A + 2 * HQ
            proj, kt, vt = _norm_matmul_kvt(xp.reshape(B, L, D), norm_mix[l], w_in_o, o, TM, k0, v0, HQ)
            yc = _pool_prompt(proj, pool_w_b[o], pool_scale[o], TL)
            yd = _band_attn(proj, q0 // LANES, k0 // LANES, v0 // LANES, DIL_PATTERNS, False)
            xp = _matmul_residual(xp, [yc.reshape(B * L, A), yd.reshape(B * L, HQ)], w_out_o, o, TM)
            pool_p.append(proj[:, L - POOL_PAD:, :A])
            dilk_p.append(kt)
            dilv_p.append(vt)

        mkv = _norm_matmul(mem, norm_mem[l], w_kv, l, TM).reshape(B, mlen, 2 * D)
        memk_p.append(mkv[:, :, :D].reshape(B, mlen, CA_HEADS, dh_ca))
        memv_p.append(mkv[:, :, D:].reshape(B, mlen, CA_HEADS, dh_ca))
        xp = _cross_prompt(xp.reshape(B, L, D), norm_ca[l], w_q, mkv, w_o, l, TL).reshape(B * L, D)

        if l % 2 == 0 and l + 1 < depth:
            yc_s, host = sample_odd_front(xs, l + 1)
            xp, yd_s = _swiglu(xp, norm_ffn[l], w_f_in, w_f_out, l, TM, g_final, host)
            xs = _matmul_residual(xs, [yc_s, yd_s.reshape(Bs, HQ)], w_out_o, (l + 1) // 2, Bs)
        elif l % 2 == 1:
            q_s = sample_cross_q(xs, l)
            begin = lambda q, k, v: _decode_mem_scores(q, k, 0, dh_ca ** -0.5)
            end = lambda weights, q, k, v, out: _decode_mem_out(weights, v, out, 0)
            specs_fn = lambda row_of, q_s=q_s, l=l: _decode_mem_specs(q_s, cache_mem_k, cache_mem_v, l, 1, row_of)
            xp, o_s = _swiglu(xp, norm_ffn[l], w_f_in, w_f_out, l, TM, g_final, (begin, end, specs_fn, Bs))
            xs = sample_ffn(xs, o_s, l)
        else:
            xp = _swiglu(xp, norm_ffn[l], w_f_in, w_f_out, l, TM, g_final)

    from_t = lambda ts: jnp.transpose(jnp.stack(ts).reshape(len(ts), B, n_q, HEAD_DIM, L), (0, 1, 4, 2, 3))
    return (xp.reshape(B, L, D), xs.reshape(Bs, 1, D),
            jnp.stack(conv_p), jnp.stack(swak_p), jnp.stack(swav_p),
            jnp.stack(pool_p), from_t(dilk_p), from_t(dilv_p),
            jnp.stack(memk_p), jnp.stack(memv_p),
            jnp.stack(conv_s), jnp.stack(swak_s), jnp.stack(swav_s),
            jnp.stack(pool_s), jnp.stack(dilk_s), jnp.stack(dilv_s))
```

```python
import functools

import jax
import jax.numpy as jnp
from jax import lax
from jax.experimental import pallas as pl
from jax.experimental.pallas import tpu as pltpu

F32 = jnp.float32
BF16 = jnp.bfloat16

LANES = 128
SUBLANES = 8
VMEM_LIMIT = 56 * 1024 * 1024

HEAD_DIM = 64
BLK = 128
CONV_W = 3
POOL_WINDOWS = (2, 4, 8, 16)
POOL_PAD = 15
POOL_GC = 128
DIL_PATTERNS = ((128, 1), (512, 4), (2048, 16))
CA_HEADS = 4
PAST_LEN = 8192
RMS_EPS = 1e-6
NEG_INF = -1e30


def _cparams(*sem):
    return pltpu.CompilerParams(dimension_semantics=sem, vmem_limit_bytes=VMEM_LIMIT)


def _rms(x, g):
    ms = jnp.mean(x * x, axis=-1, keepdims=True)
    return x * lax.rsqrt(ms + RMS_EPS) * g


def _dot(a, b):
    return jnp.dot(a, b, preferred_element_type=F32)


def _dot_nt(a, b):
    return lax.dot_general(a, b, (((1,), (1,)), ((), ())), preferred_element_type=F32)


def _resident(shape):
    return pl.BlockSpec(shape, lambda *_: (0,) * len(shape), pipeline_mode=pl.Buffered(1))


def _layer_block(stack, layer, rows=None, row_block=0):
    _, k, n = stack.shape
    return pl.BlockSpec((None, k if rows is None else rows, n), lambda *_: (layer, row_block, 0),
                        pipeline_mode=pl.Buffered(1))


def _norm_matmul_body(x_ref, g_ref, w_ref, o_ref):
    h = _rms(x_ref[...], g_ref[...]).astype(BF16)
    o_ref[...] = _dot(h, w_ref[...])


def _norm_matmul(x, g, w, layer, tm):
    m, d = x.shape
    n = w.shape[2]
    return pl.pallas_call(
        _norm_matmul_body,
        grid=(m // tm,),
        in_specs=[pl.BlockSpec((tm, d), lambda i: (i, 0)), _resident((1, d)), _layer_block(w, layer)],
        out_specs=pl.BlockSpec((tm, n), lambda i: (i, 0)),
        out_shape=jax.ShapeDtypeStruct((m, n), F32),
        compiler_params=_cparams("parallel"),
        name="norm_matmul",
    )(x, g.reshape(1, d), w)


def _norm_matmul_kvt_body(x_ref, g_ref, w_ref, o_ref, kt_ref, vt_ref, *, k0, v0, width):
    h = _rms(x_ref[0], g_ref[...]).astype(BF16)
    r = _dot(h, w_ref[...])
    o_ref[0] = r
    kt_ref[0] = r[:, k0:k0 + width].T
    vt_ref[0] = r[:, v0:v0 + width].T


def _norm_matmul_kvt(x, g, w, layer, tm, k0, v0, width):
    b, l, d = x.shape
    n = w.shape[2]
    t_spec = pl.BlockSpec((1, width, tm), lambda i, j: (i, 0, j))
    return pl.pallas_call(
        functools.partial(_norm_matmul_kvt_body, k0=k0, v0=v0, width=width),
        grid=(b, l // tm),
        in_specs=[pl.BlockSpec((1, tm, d), lambda i, j: (i, j, 0)), _resident((1, d)), _layer_block(w, layer)],
        out_specs=[pl.BlockSpec((1, tm, n), lambda i, j: (i, j, 0)), t_spec, t_spec],
        out_shape=[jax.ShapeDtypeStruct((b, l, n), F32), jax.ShapeDtypeStruct((b, width, l), F32),
                   jax.ShapeDtypeStruct((b, width, l), F32)],
        compiler_params=_cparams("parallel", "parallel"),
        name="norm_matmul_kvt",
    )(x, g.reshape(1, d), w)


def _matmul_residual_body(*refs, nparts):
    x_ref, o_ref = refs[0], refs[-1]
    acc = x_ref[...]
    for i in range(nparts):
        acc = acc + _dot(refs[1 + 2 * i][...].astype(BF16), refs[2 + 2 * i][...])
    o_ref[...] = acc


def _matmul_residual(x, parts, w, layer, tm):
    m, d = x.shape
    in_specs = [pl.BlockSpec((tm, d), lambda i: (i, 0))]
    args = [x]
    for row_block, a in enumerate(parts):
        in_specs += [pl.BlockSpec((tm, a.shape[1]), lambda i: (i, 0)), _layer_block(w, layer, a.shape[1], row_block)]
        args += [a, w]
    return pl.pallas_call(
        functools.partial(_matmul_residual_body, nparts=len(parts)),
        grid=(m // tm,),
        in_specs=in_specs,
        out_specs=pl.BlockSpec((tm, d), lambda i: (i, 0)),
        out_shape=jax.ShapeDtypeStruct((m, d), F32),
        compiler_params=_cparams("parallel"),
        name="matmul_residual",
    )(*args)


FFN_CHUNK = 256


def _swiglu_body(*refs, hidden, final, nsub, host):
    it = iter(refs)
    x_ref, g_ref, win_ref, wout_ref = next(it), next(it), next(it), next(it)
    gf_ref = next(it) if final else None
    if host is not None:
        host_in = [next(it) for _ in range(host[2])]
    o_ref = next(it)
    host_out = next(it) if host is not None else None
    h_scr, acc_scr = next(it), next(it)
    j = pl.program_id(1)

    @pl.when(j == 0)
    def _():
        x = x_ref[...]
        h_scr[...] = _rms(x, g_ref[...]).astype(BF16)
        acc_scr[...] = x

    nchunk = hidden // FFN_CHUNK
    per = -(-nchunk // nsub)
    for grp in range(nsub):
        @pl.when(j == grp)
        def _(grp=grp):
            h = h_scr[...]
            acc = acc_scr[...]
            first, last = grp * per, min((grp + 1) * per, nchunk)
            begin_at = first + 1 if last - first > 1 else first
            for c in range(first, last):
                if host is not None and c == begin_at:
                    state = host[0](*host_in)
                lo = c * FFN_CHUNK
                gate = _dot(h, win_ref[:, lo:lo + FFN_CHUNK])
                up = _dot(h, win_ref[:, hidden + lo:hidden + lo + FFN_CHUNK])
                act = (gate * (1.0 / (1.0 + jnp.exp(-gate))) * up).astype(BF16)
                acc = acc + _dot(act, wout_ref[lo:lo + FFN_CHUNK, :])
            if host is not None:
                host[1](state, *host_in, host_out)
            if grp < nsub - 1:
                acc_scr[...] = acc
            else:
                o_ref[...] = _rms(acc, gf_ref[...]) if final else acc


def _swiglu(x, g, win, wout, layer, tm, g_final=None, host=None):
    m, d = x.shape
    hidden = wout.shape[1]
    final = g_final is not None
    steps = m // tm
    in_specs = [pl.BlockSpec((tm, d), lambda i, j: (i, 0)), _resident((1, d)), _layer_block(win, layer),
                _layer_block(wout, layer)]
    args = [x, g.reshape(1, d), win, wout]
    if final:
        in_specs.append(_resident((1, d)))
        args.append(g_final.reshape(1, d))
    out_specs = [pl.BlockSpec((tm, d), lambda i, j: (i, 0))]
    out_shape = [jax.ShapeDtypeStruct((m, d), F32)]
    nsub, host_arg = 1, None
    if host is not None:
        begin_fn, end_fn, specs_fn, bs = host
        nsub = bs // steps
        h_specs, h_args, h_out_spec, h_out_shape = specs_fn(lambda i, j: i * nsub + j)
        in_specs += h_specs
        args += h_args
        out_specs.append(h_out_spec)
        out_shape.append(h_out_shape)
        host_arg = (begin_fn, end_fn, len(h_specs))
    res = pl.pallas_call(
        functools.partial(_swiglu_body, hidden=hidden, final=final, nsub=nsub, host=host_arg),
        grid=(steps, nsub),
        in_specs=in_specs,
        out_specs=out_specs,
        out_shape=out_shape,
        scratch_shapes=[pltpu.VMEM((tm, d), BF16), pltpu.VMEM((tm, d), F32)],
        compiler_params=_cparams("parallel", "arbitrary"),
        name="swiglu",
    )(*args)
    return res if host is not None else res[0]


def _conv_prompt_body(p_ref, cw_ref, ya_ref, tail_ref, ubuf, *, tl, width):
    @pl.when(pl.program_id(1) == 0)
    def _():
        ubuf[0:SUBLANES, :] = jnp.zeros((SUBLANES, width), F32)

    bg = p_ref[0, :, 0:width]
    u = p_ref[0, :, width:2 * width] * p_ref[0, :, 2 * width:3 * width]
    ubuf[SUBLANES:SUBLANES + tl, :] = u
    conv = (cw_ref[0:1, :] * ubuf[SUBLANES - 2:SUBLANES - 2 + tl, :]
            + cw_ref[1:2, :] * ubuf[SUBLANES - 1:SUBLANES - 1 + tl, :]
            + cw_ref[2:3, :] * u)
    ya_ref[0] = (bg * conv).astype(BF16)
    tail = ubuf[tl:tl + SUBLANES, :]
    tail_ref[0] = tail
    ubuf[0:SUBLANES, :] = tail


def _conv_prompt(proj, cw, tl):
    b, l, _ = proj.shape
    width = cw.shape[1]
    return pl.pallas_call(
        functools.partial(_conv_prompt_body, tl=tl, width=width),
        grid=(b, l // tl),
        in_specs=[pl.BlockSpec((1, tl, 3 * width), lambda i, j: (i, j, 0)), _resident(cw.shape)],
        out_specs=[pl.BlockSpec((1, tl, width), lambda i, j: (i, j, 0)),
                   pl.BlockSpec((1, SUBLANES, width), lambda i, j: (i, 0, 0))],
        out_shape=[jax.ShapeDtypeStruct((b, l, width), BF16), jax.ShapeDtypeStruct((b, SUBLANES, width), F32)],
        scratch_shapes=[pltpu.VMEM((tl + SUBLANES, width), F32)],
        compiler_params=_cparams("parallel", "arbitrary"),
        name="conv_prompt",
    )(proj, cw)


POOL_CARRY = 16


def _pool_prompt_body(p_ref, pw_ref, sc_ref, y_ref, ubuf, *, tl, width):
    @pl.when(pl.program_id(1) == 0)
    def _():
        ubuf[0:POOL_CARRY, :] = jnp.zeros((POOL_CARRY, width), F32)

    u = p_ref[0]
    ubuf[POOL_CARRY:POOL_CARRY + tl, :] = u
    pos = pl.program_id(1) * tl + lax.broadcasted_iota(jnp.int32, (tl, 1), 0)
    outs = []
    for g, w in enumerate(POOL_WINDOWS):
        sl = slice(g * POOL_GC, (g + 1) * POOL_GC)
        ug = u[:, sl]
        acc = ug
        for i in range(1, w):
            acc = acc + ubuf[POOL_CARRY - i:POOL_CARRY - i + tl, sl]
        cnt = jnp.minimum(pos + 1, w).astype(F32)
        dlt = (acc / cnt - ug).astype(BF16)
        outs.append(_dot(dlt, pw_ref[g]))
    y_ref[0] = (jnp.concatenate(outs, axis=1) * sc_ref[...]).astype(BF16)
    ubuf[0:POOL_CARRY, :] = ubuf[tl:tl + POOL_CARRY, :]


def _pool_prompt(proj, pw, scale, tl):
    b, l, _ = proj.shape
    width = scale.shape[0]
    return pl.pallas_call(
        functools.partial(_pool_prompt_body, tl=tl, width=width),
        grid=(b, l // tl),
        in_specs=[pl.BlockSpec((1, tl, width), lambda i, j: (i, j, 0)), _resident(pw.shape), _resident((1, width))],
        out_specs=pl.BlockSpec((1, tl, width), lambda i, j: (i, j, 0)),
        out_shape=jax.ShapeDtypeStruct((b, l, width), BF16),
        scratch_shapes=[pltpu.VMEM((tl + POOL_CARRY, width), F32)],
        compiler_params=_cparams("parallel", "arbitrary"),
        name="pool_prompt",
    )(proj, pw, scale.reshape(1, width))


TILES_IN_FLIGHT = 5


def _unroll(trips, limit=TILES_IN_FLIGHT):
    return max(u for u in range(1, max(limit, 1) + 1) if trips % u == 0)


def _band_attn_body(*refs, branches, gqa, sink, seq):
    nbr = len(branches)
    if sink:
        q_ref, k_ref, v_ref, sink_ref, y_ref = refs[:5]
        scr = refs[5:]
    else:
        q_ref, k_ref, v_ref, y_ref = refs[:4]
        scr = refs[4:]
    o_scr, l_scr = scr[:nbr], scr[nbr:2 * nbr]
    bias_first, bias_band = scr[2 * nbr], scr[2 * nbr + 1]
    lane = lax.broadcasted_iota(jnp.int32, (1, LANES), 1)
    lo = lane < HEAD_DIM

    rr = lax.broadcasted_iota(jnp.int32, (2 * BLK, 2 * BLK), 0) & (BLK - 1)
    cc = lax.broadcasted_iota(jnp.int32, (2 * BLK, 2 * BLK), 1)
    bias_band[...] = jnp.where((cc >= rr) & (cc <= rr + BLK), 0.0, NEG_INF).astype(F32)
    rr1 = lax.broadcasted_iota(jnp.int32, (2 * BLK, BLK), 0) & (BLK - 1)
    cc1 = lax.broadcasted_iota(jnp.int32, (2 * BLK, BLK), 1)
    bias_first[...] = jnp.where(cc1 <= rr1, 0.0, NEG_INF).astype(F32)

    if gqa:
        kx, vx = scr[2 * nbr + 2], scr[2 * nbr + 3]
        keep = (lane // HEAD_DIM) == (pl.program_id(1) // 2)
        chunk = 2 * BLK
        for c in range(seq // chunk):
            rows = slice(c * chunk, (c + 1) * chunk)
            kb = k_ref[0, rows, :]
            vb = v_ref[0, rows, :]
            kx[rows, :] = jnp.where(keep, kb, pltpu.roll(kb, HEAD_DIM, axis=1))
            vx[rows, :] = jnp.where(keep, vb, pltpu.roll(vb, HEAD_DIM, axis=1))
        load_k = lambda idx: kx[idx, :]
        load_v = lambda idx: vx[idx, :]
    else:
        load_k = lambda idx: k_ref[0, idx, :]
        load_v = lambda idx: v_ref[0, idx, :]
    load_q = lambda idx: q_ref[0, idx, :]

    scale = HEAD_DIM ** -0.5

    def tile(q_t, k_t, v_t, bias):
        qs = q_t * scale
        qm = jnp.concatenate([jnp.where(lo, qs, 0.0), jnp.where(lo, 0.0, qs)], axis=0).astype(BF16)
        s = _dot_nt(qm, k_t.astype(BF16)) + bias
        m = jnp.max(s, axis=1, keepdims=True)
        p = jnp.exp(s - m)
        den = jnp.sum(p, axis=1, keepdims=True)
        o2 = _dot(p.astype(BF16), v_t.astype(BF16)) / den
        lse = jnp.broadcast_to(m + jnp.log(den), (2 * BLK, LANES))
        o = jnp.where(lo, o2[:BLK], o2[BLK:])
        return o, jnp.where(lo, lse[:BLK], lse[BLK:])

    for bi, (w, d) in enumerate(branches):
        nblk = seq // (d * BLK)

        def rows(start, size, d=d):
            if isinstance(start, int):
                return pl.ds(start, size) if d == 1 else pl.ds(start, size, stride=d)
            if d == 1:
                return pl.ds(pl.multiple_of(start, BLK), size)
            return pl.ds(start, size, stride=d)

        def run_class(r, bi=bi, d=d, nblk=nblk, rows=rows):
            idx0 = rows(r, BLK)
            o, lse = tile(load_q(idx0), load_k(idx0), load_v(idx0), bias_first[...])
            o_scr[bi][idx0, :] = o
            l_scr[bi][idx0, :] = lse

            def body(i, carry):
                qi = rows(r + d * BLK * i, BLK)
                ki = rows(r + d * BLK * (i - 1), 2 * BLK)
                o, lse = tile(load_q(qi), load_k(ki), load_v(ki), bias_band[...])
                o_scr[bi][qi, :] = o
                l_scr[bi][qi, :] = lse
                return carry

            if nblk > 1:
                lax.fori_loop(1, nblk, body, 0, unroll=_unroll(nblk - 1))

        if d == 1:
            run_class(0)
        else:
            lax.fori_loop(0, d, lambda r, c, run_class=run_class: (run_class(r), c)[1], 0,
                          unroll=_unroll(d, TILES_IN_FLIGHT // min(nblk, TILES_IN_FLIGHT)))

    chunk = 2 * BLK
    for c in range(seq // chunk):
        rows_c = slice(c * chunk, (c + 1) * chunk)
        if sink:
            lse = l_scr[0][rows_c, :]
            gate = 1.0 / (1.0 + jnp.exp(sink_ref[...] - lse))
            y = o_scr[0][rows_c, :] * gate
        else:
            lses = [l_scr[bi][rows_c, :] for bi in range(nbr)]
            top = functools.reduce(jnp.maximum, lses)
            es = [jnp.exp(x - top) for x in lses]
            y = sum(e * o_scr[bi][rows_c, :] for bi, e in enumerate(es)) / sum(es)
        y_ref[0, rows_c, :] = y.astype(BF16)


def _band_attn(proj, q_blk, k_blk, v_blk, branches, gqa, sink_row=None):
    b, seq, _ = proj.shape
    nlb = 4
    nbr = len(branches)
    sink = sink_row is not None
    blk = (1, seq, LANES)
    kv_map = (lambda off: (lambda i, j: (i, 0, off))) if gqa else (lambda off: (lambda i, j: (i, 0, off + j)))
    in_specs = [pl.BlockSpec(blk, lambda i, j: (i, 0, q_blk + j)), pl.BlockSpec(blk, kv_map(k_blk)),
                pl.BlockSpec(blk, kv_map(v_blk))]
    args = [proj, proj, proj]
    if sink:
        in_specs.append(pl.BlockSpec((1, LANES), lambda i, j: (0, j)))
        args.append(sink_row)
    scratch = [pltpu.VMEM((seq, LANES), F32) for _ in range(2 * nbr)]
    scratch += [pltpu.VMEM((2 * BLK, BLK), F32), pltpu.VMEM((2 * BLK, 2 * BLK), F32)]
    if gqa:
        scratch += [pltpu.VMEM((seq, LANES), F32), pltpu.VMEM((seq, LANES), F32)]
    return pl.pallas_call(
        functools.partial(_band_attn_body, branches=branches, gqa=gqa, sink=sink, seq=seq),
        grid=(b, nlb),
        in_specs=in_specs,
        out_specs=pl.BlockSpec(blk, lambda i, j: (i, 0, j)),
        out_shape=jax.ShapeDtypeStruct((b, seq, nlb * LANES), BF16),
        scratch_shapes=scratch,
        compiler_params=_cparams("parallel", "parallel"),
        name="band_attn",
    )(*args)


def _cross_prompt_body(*refs, heads, nparts):
    x_ref = refs[0]
    g_ref, wq_ref, mkv_ref, wo_ref, o_ref = refs[1 + 2 * nparts:]
    x = x_ref[0]
    for i in range(nparts):
        x = x + _dot(refs[1 + 2 * i][0], refs[2 + 2 * i][...])
    d = x.shape[1]
    dh = d // heads
    h = _rms(x, g_ref[...]).astype(BF16)
    q = (_dot(h, wq_ref[...]) * (dh ** -0.5)).astype(BF16)
    outs = []
    for c in range(heads):
        kc = mkv_ref[0, :, c * dh:(c + 1) * dh]
        vc = mkv_ref[0, :, d + c * dh:d + (c + 1) * dh]
        s = _dot_nt(q[:, c * dh:(c + 1) * dh], kc)
        m = jnp.max(s, axis=1, keepdims=True)
        p = jnp.exp(s - m)
        den = jnp.sum(p, axis=1, keepdims=True)
        outs.append((_dot(p.astype(BF16), vc) / den).astype(BF16))
    o_ref[0] = x + _dot(jnp.concatenate(outs, axis=1), wo_ref[...])


def _cross_prompt(x, parts, w_mix, mix_layer, g, wq, mkv_all, wo, layer, tl):
    b, l, d = x.shape
    mlen = mkv_all.shape[2]
    in_specs = [pl.BlockSpec((1, tl, d), lambda i, j: (i, j, 0))]
    args = [x]
    for row_block, a in enumerate(parts):
        in_specs += [pl.BlockSpec((1, tl, a.shape[2]), lambda i, j: (i, j, 0)),
                     _layer_block(w_mix, mix_layer, a.shape[2], row_block)]
        args += [a, w_mix]
    in_specs += [_resident((1, d)), _layer_block(wq, layer),
                 pl.BlockSpec((None, 1, mlen, 2 * d), lambda i, j: (layer, i, 0, 0)), _layer_block(wo, layer)]
    args += [g.reshape(1, d), wq, mkv_all, wo]
    return pl.pallas_call(
        functools.partial(_cross_prompt_body, heads=CA_HEADS, nparts=len(parts)),
        grid=(b, l // tl),
        in_specs=in_specs,
        out_specs=pl.BlockSpec((1, tl, d), lambda i, j: (i, j, 0)),
        out_shape=jax.ShapeDtypeStruct((b, l, d), F32),
        compiler_params=_cparams("parallel", "parallel"),
        name="cross_prompt",
    )(*args)


def _memory_kv_body(x_ref, g_ref, w_ref, mkv_ref, kt_ref, vt_ref, *, heads, tm):
    h = _rms(x_ref[...], g_ref[...]).astype(BF16)
    r = _dot(h, w_ref[...])
    mkv_ref[...] = r.astype(BF16)
    d = r.shape[1] // 2
    halves = d // heads // LANES
    for t, out_ref in enumerate((kt_ref, vt_ref)):
        for hd in range(heads):
            for c in range(halves):
                col = t * d + hd * (d // heads) + c * LANES
                out_ref[pl.ds(c * heads + hd, tm, stride=heads * halves), :] = r[:, col:col + LANES]


def _memory_kv(mem, g_all, w_all, tm):
    m, d = mem.shape
    layers = w_all.shape[0]
    rows_per = d // LANES
    slab = pl.BlockSpec((None, tm * rows_per, LANES), lambda l, i: (l, i, 0))
    return pl.pallas_call(
        functools.partial(_memory_kv_body, heads=CA_HEADS, tm=tm),
        grid=(layers, m // tm),
        in_specs=[pl.BlockSpec((tm, d), lambda l, i: (i, 0)), pl.BlockSpec((None, 1, d), lambda l, i: (l, 0, 0)),
                  pl.BlockSpec((None, d, 2 * d), lambda l, i: (l, 0, 0))],
        out_specs=[pl.BlockSpec((None, tm, 2 * d), lambda l, i: (l, i, 0)), slab, slab],
        out_shape=[jax.ShapeDtypeStruct((layers, m, 2 * d), BF16),
                   jax.ShapeDtypeStruct((layers, m * rows_per, LANES), F32),
                   jax.ShapeDtypeStruct((layers, m * rows_per, LANES), F32)],
        compiler_params=_cparams("parallel", "parallel"),
        name="memory_kv",
    )(mem, g_all.reshape(layers, 1, d), w_all)


def _decode_t_scores(q_ref, knew_ref, bias_ref, sink_ref, k_ref, b, scale):
    qb = q_ref[b] * scale
    s = _dot(qb.astype(BF16), k_ref[0, b].astype(BF16))
    sn = jnp.sum(qb * knew_ref[b], axis=1, keepdims=True)
    stats = []
    for br in range(bias_ref.shape[0]):
        s_ = s + bias_ref[br]
        m = jnp.maximum(jnp.max(s_, axis=1, keepdims=True), sn)
        p = jnp.exp(s_ - m)
        pn = jnp.exp(sn - m)
        den = jnp.sum(p, axis=1, keepdims=True) + pn
        stats.append((p, pn, den, m + jnp.log(den)))
    if sink_ref is not None:
        facs = [1.0 / ((1.0 + jnp.exp(sink_ref[:, 0:1] - st[3])) * st[2]) for st in stats]
    else:
        top = functools.reduce(jnp.maximum, [st[3] for st in stats])
        es = [jnp.exp(st[3] - top) for st in stats]
        tot = sum(es)
        facs = [e / (tot * st[2]) for e, st in zip(es, stats)]
    p_all = sum(st[0] * f for st, f in zip(stats, facs))
    pn_all = sum(st[1] * f for st, f in zip(stats, facs))
    return p_all.astype(BF16), pn_all


def _decode_t_out(weights, vnew_ref, diag_ref, v_ref, o_ref, b):
    p_all, pn_all = weights
    o = _dot_nt(p_all, v_ref[0, b].astype(BF16)) + pn_all * vnew_ref[b]
    if diag_ref is not None:
        o_ref[b] = jnp.sum(o * diag_ref[...], axis=0, keepdims=True)
    else:
        o_ref[b] = o


def _decode_t_body(*refs, bt, sink, diag, scale):
    it = iter(refs)
    q_ref, knew_ref, vnew_ref, bias_ref = next(it), next(it), next(it), next(it)
    sink_ref = next(it) if sink else None
    diag_ref = next(it) if diag else None
    k_ref, v_ref, o_ref = next(it), next(it), next(it)
    weights = [_decode_t_scores(q_ref, knew_ref, bias_ref, sink_ref, k_ref, b, scale) for b in range(bt)]
    for b in range(bt):
        _decode_t_out(weights[b], vnew_ref, diag_ref, v_ref, o_ref, b)


def _decode_t_specs(q_blk, k_new, v_new, bias, kt_all, vt_all, layer, bt, sink, diag, row_of):
    bs, nh, W = q_blk.shape
    J = kt_all.shape[3]
    in_specs = [pl.BlockSpec((bt, nh, W), lambda *g: (row_of(*g), 0, 0)),
                pl.BlockSpec((bt, 1, W), lambda *g: (row_of(*g), 0, 0)),
                pl.BlockSpec((bt, 1, W), lambda *g: (row_of(*g), 0, 0)), _resident(bias.shape)]
    args = [q_blk, k_new.reshape(bs, 1, W), v_new.reshape(bs, 1, W), bias]
    for extra in (sink, diag):
        if extra is not None:
            in_specs.append(_resident(extra.shape))
            args.append(extra)
    cache_spec = pl.BlockSpec((1, bt, W, J), lambda *g: (layer, row_of(*g), 0, 0))
    in_specs += [cache_spec, cache_spec]
    args += [kt_all, vt_all]
    out_rows = 1 if diag is not None else nh
    out_spec = pl.BlockSpec((bt, out_rows, W), lambda *g: (row_of(*g), 0, 0))
    return in_specs, args, out_spec, jax.ShapeDtypeStruct((bs, out_rows, W), F32)


def _decode_t(q_blk, k_new, v_new, bias, kt_all, vt_all, layer, bt, sink=None, diag=None):
    in_specs, args, out_spec, out_shape = _decode_t_specs(q_blk, k_new, v_new, bias, kt_all, vt_all, layer, bt,
                                                          sink, diag, lambda i: i)
    return pl.pallas_call(
        functools.partial(_decode_t_body, bt=bt, sink=sink is not None, diag=diag is not None,
                          scale=HEAD_DIM ** -0.5),
        grid=(q_blk.shape[0] // bt,),
        in_specs=in_specs,
        out_specs=out_spec,
        out_shape=out_shape,
        compiler_params=_cparams("parallel"),
        name="decode_t",
    )(*args)


def _decode_mem_scores(q_ref, k_ref, b, scale):
    qb = q_ref[b] * scale
    rows = qb.shape[0]
    k3 = k_ref[0, b].reshape(-1, rows, LANES)
    part = jnp.sum(k3 * qb[None], axis=2, keepdims=True)
    s = part + pltpu.roll(part, rows // 2, axis=1)
    m = jnp.max(s, axis=0, keepdims=True)
    p = jnp.exp(s - m)
    return p / jnp.sum(p, axis=0, keepdims=True)


def _decode_mem_out(weights, v_ref, o_ref, b):
    rows = weights.shape[1]
    o_ref[b] = jnp.sum(weights * v_ref[0, b].reshape(-1, rows, LANES), axis=0)


def _decode_mem_body(q_ref, k_ref, v_ref, o_ref, *, bt, scale):
    for b in range(bt):
        _decode_mem_out(_decode_mem_scores(q_ref, k_ref, b, scale), v_ref, o_ref, b)


def _decode_mem_specs(q, k_all, v_all, layer, bt, row_of):
    bs, rows, _ = q.shape
    cache_spec = pl.BlockSpec((1, bt, k_all.shape[2], LANES), lambda *g: (layer, row_of(*g), 0, 0))
    q_spec = pl.BlockSpec((bt, rows, LANES), lambda *g: (row_of(*g), 0, 0))
    return [q_spec, cache_spec, cache_spec], [q, k_all, v_all], q_spec, jax.ShapeDtypeStruct(q.shape, F32)


def _decode_mem(q, k_all, v_all, layer, bt, scale):
    in_specs, args, out_spec, out_shape = _decode_mem_specs(q, k_all, v_all, layer, bt, lambda i: i)
    return pl.pallas_call(
        functools.partial(_decode_mem_body, bt=bt, scale=scale),
        grid=(q.shape[0] // bt,),
        in_specs=in_specs,
        out_specs=out_spec,
        out_shape=out_shape,
        compiler_params=_cparams("parallel"),
        name="decode_mem",
    )(*args)


def _conv_sample_body(p_ref, b0_ref, b1_ref, cw_ref, ya_ref, u_ref, *, width):
    bg = p_ref[:, 0:width]
    u = p_ref[:, width:2 * width] * p_ref[:, 2 * width:3 * width]
    conv = cw_ref[0:1, :] * b0_ref[...] + cw_ref[1:2, :] * b1_ref[...] + cw_ref[2:3, :] * u
    ya_ref[...] = (bg * conv).astype(BF16)
    u_ref[...] = u


def _conv_sample(proj, buf0, buf1, cw):
    bs = proj.shape[0]
    width = cw.shape[1]
    full = lambda shape: pl.BlockSpec(shape, lambda i: (0,) * len(shape))
    return pl.pallas_call(
        functools.partial(_conv_sample_body, width=width),
        grid=(1,),
        in_specs=[full((bs, 3 * width)), full((bs, width)), full((bs, width)), full(cw.shape)],
        out_specs=[full((bs, width)), full((bs, width))],
        out_shape=[jax.ShapeDtypeStruct((bs, width), BF16), jax.ShapeDtypeStruct((bs, width), F32)],
        compiler_params=_cparams("arbitrary"),
        name="conv_sample",
    )(proj, buf0, buf1, cw)


def _pool_sample_body(p_ref, buf_ref, pw_ref, sc_ref, y_ref):
    u = p_ref[...]
    outs = []
    for g, w in enumerate(POOL_WINDOWS):
        sl = slice(g * POOL_GC, (g + 1) * POOL_GC)
        ug = u[:, sl]
        acc = ug
        for i in range(1, w):
            acc = acc + buf_ref[POOL_PAD - i][:, sl]
        cnt = float(min(PAST_LEN + 1, w))
        dlt = (acc / cnt - ug).astype(BF16)
        outs.append(_dot(dlt, pw_ref[g]))
    y_ref[...] = (jnp.concatenate(outs, axis=1) * sc_ref[...]).astype(BF16)


def _pool_sample(proj, buf_t, pw, scale):
    bs = proj.shape[0]
    width = scale.shape[0]
    full = lambda shape: pl.BlockSpec(shape, lambda i: (0,) * len(shape))
    return pl.pallas_call(
        _pool_sample_body,
        grid=(1,),
        in_specs=[full((bs, width)), full(buf_t.shape), full(pw.shape), full((1, width))],
        out_specs=full((bs, width)),
        out_shape=jax.ShapeDtypeStruct((bs, width), BF16),
        compiler_params=_cparams("arbitrary"),
        name="pool_sample",
    )(proj, buf_t, pw, scale.reshape(1, width))


def kernel(x_prompt, x_sample, mem_prompt, cache_conv, cache_swa_k, cache_swa_v, state_pool, cache_dil_k, cache_dil_v, cache_mem_k, cache_mem_v, norm_mix, norm_ca, norm_mem, norm_ffn, norm_final, w_in_even, conv_w, sinks, w_out_even, w_in_odd, pool_w, pool_scale, w_out_odd, w_ca_q, w_ca_kv, w_ca_o, w_ffn_in, w_ffn_out):
    B, L, D = x_prompt.shape
    Bs = x_sample.shape[0]
    depth = norm_mix.shape[0]
    mlen = mem_prompt.shape[1]
    A = conv_w.shape[2]
    HQ = D // 2
    n_q = HQ // HEAD_DIM
    n_kv = cache_swa_k.shape[3]
    rep = n_q // n_kv
    KV = n_kv * HEAD_DIM
    dh_ca = D // CA_HEADS
    TM = 512
    TL = 512

    w_in_e, w_out_e, w_in_o, w_out_o = (w.astype(BF16) for w in (w_in_even, w_out_even, w_in_odd, w_out_odd))
    w_q, w_kv, w_o, w_f_in, w_f_out = (w.astype(BF16) for w in (w_ca_q, w_ca_kv, w_ca_o, w_ffn_in, w_ffn_out))
    pool_w_b = pool_w.astype(BF16)

    xp = x_prompt.reshape(B * L, D)
    xs = x_sample.reshape(Bs, D)

    halves = dh_ca // LANES
    slab_rows = CA_HEADS * halves
    to_slab = lambda a: a.reshape(a.shape[:-2] + (CA_HEADS, halves, LANES)).swapaxes(-3, -2)
    from_slab = lambda a: a.swapaxes(-3, -2).reshape(a.shape[:-3] + (CA_HEADS, dh_ca))
    mkv_all, memk_t, memv_t = _memory_kv(mem_prompt.reshape(B * mlen, D), norm_mem, w_kv, TM)
    mkv_all = mkv_all.reshape(depth, B, mlen, 2 * D)
    memk_p = from_slab(memk_t.reshape(depth, B, mlen, halves, CA_HEADS, LANES))
    memv_p = from_slab(memv_t.reshape(depth, B, mlen, halves, CA_HEADS, LANES))
    mem_kc = to_slab(cache_mem_k).reshape(depth, Bs, mlen * slab_rows, LANES)
    mem_vc = to_slab(cache_mem_v).reshape(depth, Bs, mlen * slab_rows, LANES)

    swa_len, dil_len = cache_swa_k.shape[2], cache_dil_k.shape[2]
    to_t = lambda c: jnp.transpose(c, (0, 1, 3, 4, 2)).reshape(c.shape[0], Bs, c.shape[3] * HEAD_DIM, c.shape[2])
    swa_kt, swa_vt, dil_kt, dil_vt = to_t(cache_swa_k), to_t(cache_swa_v), to_t(cache_dil_k), to_t(cache_dil_v)

    pos = jnp.arange(dil_len)
    dil_bias = jnp.stack([jnp.where((pos >= dil_len - w) & ((dil_len - pos) % d == 0), 0.0, NEG_INF)
                          for w, d in DIL_PATTERNS]).astype(F32).reshape(len(DIL_PATTERNS), 1, dil_len)
    swa_bias = jnp.zeros((1, 1, swa_len), F32)
    head_of_row = jnp.arange(n_q)
    dil_diag = jnp.repeat(jnp.eye(n_q, dtype=F32), HEAD_DIM, axis=1)
    swa_diag = jnp.repeat(jax.nn.one_hot(head_of_row // rep, n_kv, dtype=F32), HEAD_DIM, axis=1)

    conv_p, swak_p, swav_p, pool_p, dilk_p, dilv_p = [], [], [], [], [], []
    conv_s, swak_s, swav_s, pool_s, dilk_s, dilv_s = [], [], [], [], [], []
    attn_scale = HEAD_DIM ** -0.5

    def sample_cross_q(xs, l):
        q = _norm_matmul(xs, norm_ca[l], w_q, l, Bs).reshape(Bs, CA_HEADS, dh_ca)
        return to_slab(q).reshape(Bs, slab_rows, LANES)

    def sample_ffn(xs, o_s, l):
        o_s = from_slab(o_s.reshape(Bs, halves, CA_HEADS, LANES)).reshape(Bs, D)
        xs = _matmul_residual(xs, [o_s], w_o, l, Bs)
        return _swiglu(xs, norm_ffn[l], w_f_in, w_f_out, l, Bs, norm_final if l == depth - 1 else None)

    def sample_odd_front(xs, l):
        o = l // 2
        q0, k0, v0 = A, A + HQ, A + 2 * HQ
        proj_s = _norm_matmul(xs, norm_mix[l], w_in_o, o, Bs)
        yc_s = _pool_sample(proj_s, state_pool[o].transpose(1, 0, 2), pool_w_b[o], pool_scale[o])
        pool_s.append(jnp.concatenate([state_pool[o][:, 1:], proj_s[:, None, :A]], axis=1))
        k_new, v_new = proj_s[:, k0:k0 + HQ], proj_s[:, v0:v0 + HQ]
        dilk_s.append(k_new.reshape(Bs, 1, n_q, HEAD_DIM))
        dilv_s.append(v_new.reshape(Bs, 1, n_q, HEAD_DIM))
        q_blk = proj_s[:, None, q0:q0 + HQ] * dil_diag[None]
        begin = lambda q, kn, vn, bias, diag, k, v: _decode_t_scores(q, kn, bias, None, k, 0, attn_scale)
        end = lambda weights, q, kn, vn, bias, diag, k, v, out: _decode_t_out(weights, vn, diag, v, out, 0)
        specs_fn = lambda row_of: _decode_t_specs(q_blk, k_new, v_new, dil_bias, dil_kt, dil_vt, o, 1, None,
                                                  dil_diag, row_of)
        return yc_s, (begin, end, specs_fn, Bs)

    for l in range(depth):
        g_final = norm_final if l == depth - 1 else None
        if l % 2 == 0:
            e = l // 2
            q0, k0, v0 = 3 * A, 3 * A + HQ, 3 * A + HQ + KV
            proj = _norm_matmul(xp, norm_mix[l], w_in_e, e, TM).reshape(B, L, -1)
            ya, tail = _conv_prompt(proj, conv_w[e], TL)
            sink_row = jnp.repeat(sinks[e], HEAD_DIM).reshape(1, HQ)
            yb = _band_attn(proj, q0 // LANES, k0 // LANES, v0 // LANES, ((BLK, 1),), True, sink_row)
            mix_parts, w_mix, mix_layer = [ya, yb], w_out_e, e
            conv_p.append(tail[:, SUBLANES - (CONV_W - 1):])
            swak_p.append(proj[:, L - BLK:, k0:k0 + KV].reshape(B, BLK, n_kv, HEAD_DIM))
            swav_p.append(proj[:, L - BLK:, v0:v0 + KV].reshape(B, BLK, n_kv, HEAD_DIM))
            proj_s = _norm_matmul(xs, norm_mix[l], w_in_e, e, Bs)
            ya_s, u_s = _conv_sample(proj_s, cache_conv[e][:, 0], cache_conv[e][:, 1], conv_w[e])
            conv_s.append(jnp.stack([cache_conv[e][:, 1], u_s], axis=1))
            k_new, v_new = proj_s[:, k0:k0 + KV], proj_s[:, v0:v0 + KV]
            q_h = proj_s[:, q0:q0 + HQ].reshape(Bs, n_q, 1, HEAD_DIM)
            q_blk = (q_h * swa_diag.reshape(n_q, n_kv, HEAD_DIM)[None]).reshape(Bs, n_q, KV)
            sink_col = jnp.broadcast_to(sinks[e][:, None], (n_q, LANES))
            o_raw = _decode_t(q_blk, k_new, v_new, swa_bias, swa_kt, swa_vt, e, 8, sink=sink_col)
            o_raw = o_raw.reshape(Bs, n_kv, rep, n_kv, HEAD_DIM)
            yb_s = jnp.stack([o_raw[:, g, :, g, :] for g in range(n_kv)], axis=1).reshape(Bs, HQ)
            xs = _matmul_residual(xs, [ya_s, yb_s], w_out_e, e, Bs)
            swak_s.append(k_new.reshape(Bs, 1, n_kv, HEAD_DIM))
            swav_s.append(v_new.reshape(Bs, 1, n_kv, HEAD_DIM))
            xs = sample_ffn(xs, _decode_mem(sample_cross_q(xs, l), mem_kc, mem_vc, l, 8, dh_ca ** -0.5), l)
        else:
            o = l // 2
            q0, k0, v0 = A, A + HQ, A + 2 * HQ
            proj, kt, vt = _norm_matmul_kvt(xp.reshape(B, L, D), norm_mix[l], w_in_o, o, TM, k0, v0, HQ)
            yc = _pool_prompt(proj, pool_w_b[o], pool_scale[o], TL)
            yd = _band_attn(proj, q0 // LANES, k0 // LANES, v0 // LANES, DIL_PATTERNS, False)
            mix_parts, w_mix, mix_layer = [yc, yd], w_out_o, o
            pool_p.append(proj[:, L - POOL_PAD:, :A])
            dilk_p.append(kt)
            dilv_p.append(vt)

        xp = _cross_prompt(xp.reshape(B, L, D), mix_parts, w_mix, mix_layer, norm_ca[l], w_q, mkv_all, w_o, l,
                           TL).reshape(B * L, D)

        if l % 2 == 0 and l + 1 < depth:
            yc_s, host = sample_odd_front(xs, l + 1)
            xp, yd_s = _swiglu(xp, norm_ffn[l], w_f_in, w_f_out, l, TM, g_final, host)
            xs = _matmul_residual(xs, [yc_s, yd_s.reshape(Bs, HQ)], w_out_o, (l + 1) // 2, Bs)
        elif l % 2 == 1:
            q_s = sample_cross_q(xs, l)
            begin = lambda q, k, v: _decode_mem_scores(q, k, 0, dh_ca ** -0.5)
            end = lambda weights, q, k, v, out: _decode_mem_out(weights, v, out, 0)
            specs_fn = lambda row_of, q_s=q_s, l=l: _decode_mem_specs(q_s, mem_kc, mem_vc, l, 1, row_of)
            xp, o_s = _swiglu(xp, norm_ffn[l], w_f_in, w_f_out, l, TM, g_final, (begin, end, specs_fn, Bs))
            xs = sample_ffn(xs, o_s, l)
        else:
            xp = _swiglu(xp, norm_ffn[l], w_f_in, w_f_out, l, TM, g_final)

    from_t = lambda ts: jnp.transpose(jnp.stack(ts).reshape(len(ts), B, n_q, HEAD_DIM, L), (0, 1, 4, 2, 3))
    return (xp.reshape(B, L, D), xs.reshape(Bs, 1, D),
            jnp.stack(conv_p), jnp.stack(swak_p), jnp.stack(swav_p),
            jnp.stack(pool_p), from_t(dilk_p), from_t(dilv_p),
            memk_p, memv_p,
            jnp.stack(conv_s), jnp.stack(swak_s), jnp.stack(swav_s),
            jnp.stack(pool_s), jnp.stack(dilk_s), jnp.stack(dilv_s))
```

```python
import functools

import jax
import jax.numpy as jnp
from jax import lax
from jax.experimental import pallas as pl
from jax.experimental.pallas import tpu as pltpu

F32 = jnp.float32
BF16 = jnp.bfloat16

LANES = 128
SUBLANES = 8
VMEM_LIMIT = 56 * 1024 * 1024

HEAD_DIM = 64
BLK = 128
CONV_W = 3
POOL_WINDOWS = (2, 4, 8, 16)
POOL_PAD = 15
POOL_GC = 128
DIL_PATTERNS = ((128, 1), (512, 4), (2048, 16))
CA_HEADS = 4
PAST_LEN = 8192
RMS_EPS = 1e-6
NEG_INF = -1e30


def _cparams(*sem):
    return pltpu.CompilerParams(dimension_semantics=sem, vmem_limit_bytes=VMEM_LIMIT)


def _rms(x, g):
    ms = jnp.mean(x * x, axis=-1, keepdims=True)
    return x * lax.rsqrt(ms + RMS_EPS) * g


def _dot(a, b):
    return jnp.dot(a, b, preferred_element_type=F32)


def _dot_nt(a, b):
    return lax.dot_general(a, b, (((1,), (1,)), ((), ())), preferred_element_type=F32)


def _resident(shape):
    return pl.BlockSpec(shape, lambda *_: (0,) * len(shape), pipeline_mode=pl.Buffered(1))


def _layer_block(stack, layer, rows=None, row_block=0):
    _, k, n = stack.shape
    return pl.BlockSpec((None, k if rows is None else rows, n), lambda *_: (layer, row_block, 0),
                        pipeline_mode=pl.Buffered(1))


def _norm_matmul_body(x_ref, g_ref, w_ref, o_ref):
    h = _rms(x_ref[...], g_ref[...]).astype(BF16)
    o_ref[...] = _dot(h, w_ref[...])


def _norm_matmul(x, g, w, layer, tm):
    m, d = x.shape
    n = w.shape[2]
    return pl.pallas_call(
        _norm_matmul_body,
        grid=(m // tm,),
        in_specs=[pl.BlockSpec((tm, d), lambda i: (i, 0)), _resident((1, d)), _layer_block(w, layer)],
        out_specs=pl.BlockSpec((tm, n), lambda i: (i, 0)),
        out_shape=jax.ShapeDtypeStruct((m, n), F32),
        compiler_params=_cparams("parallel"),
        name="norm_matmul",
    )(x, g.reshape(1, d), w)


def _matmul_residual_body(*refs, nparts):
    x_ref, o_ref = refs[0], refs[-1]
    acc = x_ref[...]
    for i in range(nparts):
        acc = acc + _dot(refs[1 + 2 * i][...].astype(BF16), refs[2 + 2 * i][...])
    o_ref[...] = acc


def _matmul_residual(x, parts, w, layer, tm):
    m, d = x.shape
    in_specs = [pl.BlockSpec((tm, d), lambda i: (i, 0))]
    args = [x]
    for row_block, a in enumerate(parts):
        in_specs += [pl.BlockSpec((tm, a.shape[1]), lambda i: (i, 0)), _layer_block(w, layer, a.shape[1], row_block)]
        args += [a, w]
    return pl.pallas_call(
        functools.partial(_matmul_residual_body, nparts=len(parts)),
        grid=(m // tm,),
        in_specs=in_specs,
        out_specs=pl.BlockSpec((tm, d), lambda i: (i, 0)),
        out_shape=jax.ShapeDtypeStruct((m, d), F32),
        compiler_params=_cparams("parallel"),
        name="matmul_residual",
    )(*args)


FFN_CHUNK = 256


def _swiglu_body(*refs, hidden, final, nsub, host):
    it = iter(refs)
    x_ref, g_ref, win_ref, wout_ref = next(it), next(it), next(it), next(it)
    gf_ref = next(it) if final else None
    if host is not None:
        host_in = [next(it) for _ in range(host[2])]
    o_ref = next(it)
    host_out = next(it) if host is not None else None
    h_scr, acc_scr = next(it), next(it)
    j = pl.program_id(1)

    @pl.when(j == 0)
    def _():
        x = x_ref[...]
        h_scr[...] = _rms(x, g_ref[...]).astype(BF16)
        acc_scr[...] = x

    nchunk = hidden // FFN_CHUNK
    per = -(-nchunk // nsub)
    for grp in range(nsub):
        @pl.when(j == grp)
        def _(grp=grp):
            h = h_scr[...]
            acc = acc_scr[...]
            first, last = grp * per, min((grp + 1) * per, nchunk)
            begin_at = first + 1 if last - first > 1 else first
            for c in range(first, last):
                if host is not None and c == begin_at:
                    state = host[0](*host_in, 0)
                lo = c * FFN_CHUNK
                gate = _dot(h, win_ref[:, lo:lo + FFN_CHUNK])
                up = _dot(h, win_ref[:, hidden + lo:hidden + lo + FFN_CHUNK])
                act = (gate * (1.0 / (1.0 + jnp.exp(-gate))) * up).astype(BF16)
                acc = acc + _dot(act, wout_ref[lo:lo + FFN_CHUNK, :])
            if host is not None:
                host[1](state, *host_in, host_out, 0)
            if grp < nsub - 1:
                acc_scr[...] = acc
            else:
                o_ref[...] = _rms(acc, gf_ref[...]) if final else acc


def _swiglu(x, g, win, wout, layer, tm, g_final=None, host=None):
    m, d = x.shape
    hidden = wout.shape[1]
    final = g_final is not None
    steps = m // tm
    in_specs = [pl.BlockSpec((tm, d), lambda i, j: (i, 0)), _resident((1, d)), _layer_block(win, layer),
                _layer_block(wout, layer)]
    args = [x, g.reshape(1, d), win, wout]
    if final:
        in_specs.append(_resident((1, d)))
        args.append(g_final.reshape(1, d))
    out_specs = [pl.BlockSpec((tm, d), lambda i, j: (i, 0))]
    out_shape = [jax.ShapeDtypeStruct((m, d), F32)]
    nsub, host_arg = 1, None
    if host is not None:
        begin_fn, end_fn, specs_fn, bs = host
        nsub = bs // steps
        h_specs, h_args, h_out_spec, h_out_shape = specs_fn(1, lambda i, j: i * nsub + j)
        in_specs += h_specs
        args += h_args
        out_specs.append(h_out_spec)
        out_shape.append(h_out_shape)
        host_arg = (begin_fn, end_fn, len(h_specs))
    res = pl.pallas_call(
        functools.partial(_swiglu_body, hidden=hidden, final=final, nsub=nsub, host=host_arg),
        grid=(steps, nsub),
        in_specs=in_specs,
        out_specs=out_specs,
        out_shape=out_shape,
        scratch_shapes=[pltpu.VMEM((tm, d), BF16), pltpu.VMEM((tm, d), F32)],
        compiler_params=_cparams("parallel", "arbitrary"),
        name="swiglu",
    )(*args)
    return res if host is not None else res[0]


def _inproj_even_body(x_ref, g_ref, w_ref, cw_ref, ya_ref, qkv_ref, tail_ref, ubuf, *, tm, width):
    @pl.when(pl.program_id(1) == 0)
    def _():
        ubuf[0:SUBLANES, :] = jnp.zeros((SUBLANES, width), F32)

    h = _rms(x_ref[0], g_ref[...]).astype(BF16)
    r = _dot(h, w_ref[...])
    qkv_ref[0] = r[:, 3 * width:]
    u = r[:, width:2 * width] * r[:, 2 * width:3 * width]
    ubuf[SUBLANES:SUBLANES + tm, :] = u
    conv = (cw_ref[0:1, :] * ubuf[SUBLANES - 2:SUBLANES - 2 + tm, :]
            + cw_ref[1:2, :] * ubuf[SUBLANES - 1:SUBLANES - 1 + tm, :]
            + cw_ref[2:3, :] * u)
    ya_ref[0] = (r[:, 0:width] * conv).astype(BF16)
    tail = ubuf[tm:tm + SUBLANES, :]
    tail_ref[0] = tail
    ubuf[0:SUBLANES, :] = tail


def _inproj_even(x, g, w, layer, cw, tm):
    b, l, d = x.shape
    n = w.shape[2]
    width = cw.shape[1]
    rest = n - 3 * width
    return pl.pallas_call(
        functools.partial(_inproj_even_body, tm=tm, width=width),
        grid=(b, l // tm),
        in_specs=[pl.BlockSpec((1, tm, d), lambda i, j: (i, j, 0)), _resident((1, d)), _layer_block(w, layer),
                  _resident(cw.shape)],
        out_specs=[pl.BlockSpec((1, tm, width), lambda i, j: (i, j, 0)),
                   pl.BlockSpec((1, tm, rest), lambda i, j: (i, j, 0)),
                   pl.BlockSpec((1, SUBLANES, width), lambda i, j: (i, 0, 0))],
        out_shape=[jax.ShapeDtypeStruct((b, l, width), BF16), jax.ShapeDtypeStruct((b, l, rest), F32),
                   jax.ShapeDtypeStruct((b, SUBLANES, width), F32)],
        scratch_shapes=[pltpu.VMEM((tm + SUBLANES, width), F32)],
        compiler_params=_cparams("parallel", "arbitrary"),
        name="inproj_even",
    )(x, g.reshape(1, d), w, cw)


POOL_CARRY = 16


def _inproj_odd_body(x_ref, g_ref, w_ref, pw_ref, sc_ref, yc_ref, qkv_ref, kt_ref, vt_ref, tail_ref, ubuf, *, tm, width):
    @pl.when(pl.program_id(1) == 0)
    def _():
        ubuf[0:POOL_CARRY, :] = jnp.zeros((POOL_CARRY, width), F32)

    h = _rms(x_ref[0], g_ref[...]).astype(BF16)
    r = _dot(h, w_ref[...])
    qkv_ref[0] = r[:, width:]
    kt_ref[0] = r[:, 2 * width:3 * width].T
    vt_ref[0] = r[:, 3 * width:4 * width].T
    u = r[:, 0:width]
    ubuf[POOL_CARRY:POOL_CARRY + tm, :] = u
    pos = pl.program_id(1) * tm + lax.broadcasted_iota(jnp.int32, (tm, 1), 0)
    outs = []
    for g, w in enumerate(POOL_WINDOWS):
        sl = slice(g * POOL_GC, (g + 1) * POOL_GC)
        ug = u[:, sl]
        acc = ug
        for i in range(1, w):
            acc = acc + ubuf[POOL_CARRY - i:POOL_CARRY - i + tm, sl]
        cnt = jnp.minimum(pos + 1, w).astype(F32)
        dlt = (acc / cnt - ug).astype(BF16)
        outs.append(_dot(dlt, pw_ref[g]))
    yc_ref[0] = (jnp.concatenate(outs, axis=1) * sc_ref[...]).astype(BF16)
    tail = ubuf[tm:tm + POOL_CARRY, :]
    tail_ref[0] = tail
    ubuf[0:POOL_CARRY, :] = tail


def _inproj_odd(x, g, w, layer, pw, scale, tm):
    b, l, d = x.shape
    n = w.shape[2]
    width = scale.shape[0]
    t_spec = pl.BlockSpec((1, width, tm), lambda i, j: (i, 0, j))
    return pl.pallas_call(
        functools.partial(_inproj_odd_body, tm=tm, width=width),
        grid=(b, l // tm),
        in_specs=[pl.BlockSpec((1, tm, d), lambda i, j: (i, j, 0)), _resident((1, d)), _layer_block(w, layer),
                  _resident(pw.shape), _resident((1, width))],
        out_specs=[pl.BlockSpec((1, tm, width), lambda i, j: (i, j, 0)),
                   pl.BlockSpec((1, tm, n - width), lambda i, j: (i, j, 0)), t_spec, t_spec,
                   pl.BlockSpec((1, POOL_CARRY, width), lambda i, j: (i, 0, 0))],
        out_shape=[jax.ShapeDtypeStruct((b, l, width), BF16), jax.ShapeDtypeStruct((b, l, n - width), F32),
                   jax.ShapeDtypeStruct((b, width, l), F32), jax.ShapeDtypeStruct((b, width, l), F32),
                   jax.ShapeDtypeStruct((b, POOL_CARRY, width), F32)],
        scratch_shapes=[pltpu.VMEM((tm + POOL_CARRY, width), F32)],
        compiler_params=_cparams("parallel", "arbitrary"),
        name="inproj_odd",
    )(x, g.reshape(1, d), w, pw, scale.reshape(1, width))


TILES_IN_FLIGHT = 5


def _unroll(trips, limit=TILES_IN_FLIGHT):
    return max(u for u in range(1, max(limit, 1) + 1) if trips % u == 0)


def _band_attn_body(*refs, branches, gqa, sink, seq):
    nbr = len(branches)
    if sink:
        q_ref, k_ref, v_ref, sink_ref, y_ref = refs[:5]
        scr = refs[5:]
    else:
        q_ref, k_ref, v_ref, y_ref = refs[:4]
        scr = refs[4:]
    o_scr, l_scr = scr[:nbr], scr[nbr:2 * nbr]
    bias_first, bias_band = scr[2 * nbr], scr[2 * nbr + 1]
    lane = lax.broadcasted_iota(jnp.int32, (1, LANES), 1)
    lo = lane < HEAD_DIM

    rr = lax.broadcasted_iota(jnp.int32, (2 * BLK, 2 * BLK), 0) & (BLK - 1)
    cc = lax.broadcasted_iota(jnp.int32, (2 * BLK, 2 * BLK), 1)
    bias_band[...] = jnp.where((cc >= rr) & (cc <= rr + BLK), 0.0, NEG_INF).astype(F32)
    rr1 = lax.broadcasted_iota(jnp.int32, (2 * BLK, BLK), 0) & (BLK - 1)
    cc1 = lax.broadcasted_iota(jnp.int32, (2 * BLK, BLK), 1)
    bias_first[...] = jnp.where(cc1 <= rr1, 0.0, NEG_INF).astype(F32)

    if gqa:
        kx, vx = scr[2 * nbr + 2], scr[2 * nbr + 3]
        keep = (lane // HEAD_DIM) == (pl.program_id(1) // 2)
        chunk = 2 * BLK
        for c in range(seq // chunk):
            rows = slice(c * chunk, (c + 1) * chunk)
            kb = k_ref[0, rows, :]
            vb = v_ref[0, rows, :]
            kx[rows, :] = jnp.where(keep, kb, pltpu.roll(kb, HEAD_DIM, axis=1))
            vx[rows, :] = jnp.where(keep, vb, pltpu.roll(vb, HEAD_DIM, axis=1))
        load_k = lambda idx: kx[idx, :]
        load_v = lambda idx: vx[idx, :]
    else:
        load_k = lambda idx: k_ref[0, idx, :]
        load_v = lambda idx: v_ref[0, idx, :]
    load_q = lambda idx: q_ref[0, idx, :]

    scale = HEAD_DIM ** -0.5

    def tile(q_t, k_t, v_t, bias):
        qs = q_t * scale
        qm = jnp.concatenate([jnp.where(lo, qs, 0.0), jnp.where(lo, 0.0, qs)], axis=0).astype(BF16)
        s = _dot_nt(qm, k_t.astype(BF16)) + bias
        m = jnp.max(s, axis=1, keepdims=True)
        p = jnp.exp(s - m)
        den = jnp.sum(p, axis=1, keepdims=True)
        o2 = _dot(p.astype(BF16), v_t.astype(BF16)) / den
        lse = jnp.broadcast_to(m + jnp.log(den), (2 * BLK, LANES))
        o = jnp.where(lo, o2[:BLK], o2[BLK:])
        return o, jnp.where(lo, lse[:BLK], lse[BLK:])

    for bi, (w, d) in enumerate(branches):
        nblk = seq // (d * BLK)

        def rows(start, size, d=d):
            if isinstance(start, int):
                return pl.ds(start, size) if d == 1 else pl.ds(start, size, stride=d)
            if d == 1:
                return pl.ds(pl.multiple_of(start, BLK), size)
            return pl.ds(start, size, stride=d)

        def run_class(r, bi=bi, d=d, nblk=nblk, rows=rows):
            idx0 = rows(r, BLK)
            o, lse = tile(load_q(idx0), load_k(idx0), load_v(idx0), bias_first[...])
            o_scr[bi][idx0, :] = o
            l_scr[bi][idx0, :] = lse

            def body(i, carry):
                qi = rows(r + d * BLK * i, BLK)
                ki = rows(r + d * BLK * (i - 1), 2 * BLK)
                o, lse = tile(load_q(qi), load_k(ki), load_v(ki), bias_band[...])
                o_scr[bi][qi, :] = o
                l_scr[bi][qi, :] = lse
                return carry

            if nblk > 1:
                lax.fori_loop(1, nblk, body, 0, unroll=_unroll(nblk - 1))

        if d == 1:
            run_class(0)
        else:
            lax.fori_loop(0, d, lambda r, c, run_class=run_class: (run_class(r), c)[1], 0,
                          unroll=_unroll(d, TILES_IN_FLIGHT // min(nblk, TILES_IN_FLIGHT)))

    chunk = 2 * BLK
    for c in range(seq // chunk):
        rows_c = slice(c * chunk, (c + 1) * chunk)
        if sink:
            lse = l_scr[0][rows_c, :]
            gate = 1.0 / (1.0 + jnp.exp(sink_ref[...] - lse))
            y = o_scr[0][rows_c, :] * gate
        else:
            lses = [l_scr[bi][rows_c, :] for bi in range(nbr)]
            top = functools.reduce(jnp.maximum, lses)
            es = [jnp.exp(x - top) for x in lses]
            y = sum(e * o_scr[bi][rows_c, :] for bi, e in enumerate(es)) / sum(es)
        y_ref[0, rows_c, :] = y.astype(BF16)


def _band_attn(proj, q_blk, k_blk, v_blk, branches, gqa, sink_row=None):
    b, seq, _ = proj.shape
    nlb = 4
    nbr = len(branches)
    sink = sink_row is not None
    blk = (1, seq, LANES)
    kv_map = (lambda off: (lambda i, j: (i, 0, off))) if gqa else (lambda off: (lambda i, j: (i, 0, off + j)))
    in_specs = [pl.BlockSpec(blk, lambda i, j: (i, 0, q_blk + j)), pl.BlockSpec(blk, kv_map(k_blk)),
                pl.BlockSpec(blk, kv_map(v_blk))]
    args = [proj, proj, proj]
    if sink:
        in_specs.append(pl.BlockSpec((1, LANES), lambda i, j: (0, j)))
        args.append(sink_row)
    scratch = [pltpu.VMEM((seq, LANES), F32) for _ in range(2 * nbr)]
    scratch += [pltpu.VMEM((2 * BLK, BLK), F32), pltpu.VMEM((2 * BLK, 2 * BLK), F32)]
    if gqa:
        scratch += [pltpu.VMEM((seq, LANES), F32), pltpu.VMEM((seq, LANES), F32)]
    return pl.pallas_call(
        functools.partial(_band_attn_body, branches=branches, gqa=gqa, sink=sink, seq=seq),
        grid=(b, nlb),
        in_specs=in_specs,
        out_specs=pl.BlockSpec(blk, lambda i, j: (i, 0, j)),
        out_shape=jax.ShapeDtypeStruct((b, seq, nlb * LANES), BF16),
        scratch_shapes=scratch,
        compiler_params=_cparams("parallel", "parallel"),
        name="band_attn",
    )(*args)


def _cross_prompt_body(*refs, heads, nparts, host):
    x_ref = refs[0]
    g_ref, wq_ref, mkv_ref, wo_ref = refs[1 + 2 * nparts:5 + 2 * nparts]
    rest = refs[5 + 2 * nparts:]
    if host is not None:
        begin_fn, end_fn, n_in, bt = host
        host_in, o_ref, host_out = rest[:n_in], rest[n_in], rest[n_in + 1]
        states = [begin_fn(*host_in, b) for b in range(bt)]
    else:
        o_ref = rest[0]
    x = x_ref[0]
    for i in range(nparts):
        x = x + _dot(refs[1 + 2 * i][0], refs[2 + 2 * i][...])
    d = x.shape[1]
    dh = d // heads
    h = _rms(x, g_ref[...]).astype(BF16)
    q = (_dot(h, wq_ref[...]) * (dh ** -0.5)).astype(BF16)
    outs = []
    for c in range(heads):
        kc = mkv_ref[0, :, c * dh:(c + 1) * dh]
        vc = mkv_ref[0, :, d + c * dh:d + (c + 1) * dh]
        s = _dot_nt(q[:, c * dh:(c + 1) * dh], kc)
        m = jnp.max(s, axis=1, keepdims=True)
        p = jnp.exp(s - m)
        den = jnp.sum(p, axis=1, keepdims=True)
        outs.append((_dot(p.astype(BF16), vc) / den).astype(BF16))
    o_ref[0] = x + _dot(jnp.concatenate(outs, axis=1), wo_ref[...])
    if host is not None:
        for b in range(bt):
            end_fn(states[b], *host_in, host_out, b)


def _cross_prompt(x, parts, w_mix, mix_layer, g, wq, mkv_all, wo, layer, tl, host=None):
    b, l, d = x.shape
    mlen = mkv_all.shape[2]
    in_specs = [pl.BlockSpec((1, tl, d), lambda i, j: (i, j, 0))]
    args = [x]
    for row_block, a in enumerate(parts):
        in_specs += [pl.BlockSpec((1, tl, a.shape[2]), lambda i, j: (i, j, 0)),
                     _layer_block(w_mix, mix_layer, a.shape[2], row_block)]
        args += [a, w_mix]
    in_specs += [_resident((1, d)), _layer_block(wq, layer),
                 pl.BlockSpec((None, 1, mlen, 2 * d), lambda i, j: (layer, i, 0, 0)), _layer_block(wo, layer)]
    args += [g.reshape(1, d), wq, mkv_all, wo]
    out_specs = [pl.BlockSpec((1, tl, d), lambda i, j: (i, j, 0))]
    out_shape = [jax.ShapeDtypeStruct((b, l, d), F32)]
    host_arg = None
    if host is not None:
        begin_fn, end_fn, specs_fn, rows = host
        inner = l // tl
        bt = rows // (b * inner)
        h_specs, h_args, h_out_spec, h_out_shape = specs_fn(bt, lambda i, j: i * inner + j)
        in_specs += h_specs
        args += h_args
        out_specs.append(h_out_spec)
        out_shape.append(h_out_shape)
        host_arg = (begin_fn, end_fn, len(h_specs), bt)
    res = pl.pallas_call(
        functools.partial(_cross_prompt_body, heads=CA_HEADS, nparts=len(parts), host=host_arg),
        grid=(b, l // tl),
        in_specs=in_specs,
        out_specs=out_specs,
        out_shape=out_shape,
        compiler_params=_cparams("parallel", "parallel"),
        name="cross_prompt",
    )(*args)
    return res if host is not None else res[0]


def _memory_kv_body(x_ref, g_ref, w_ref, mkv_ref, kt_ref, vt_ref, *, heads, tm):
    h = _rms(x_ref[...], g_ref[...]).astype(BF16)
    r = _dot(h, w_ref[...])
    mkv_ref[...] = r.astype(BF16)
    d = r.shape[1] // 2
    halves = d // heads // LANES
    for t, out_ref in enumerate((kt_ref, vt_ref)):
        for hd in range(heads):
            for c in range(halves):
                col = t * d + hd * (d // heads) + c * LANES
                out_ref[pl.ds(c * heads + hd, tm, stride=heads * halves), :] = r[:, col:col + LANES]


def _memory_kv(mem, g_all, w_all, tm):
    m, d = mem.shape
    layers = w_all.shape[0]
    rows_per = d // LANES
    slab = pl.BlockSpec((None, tm * rows_per, LANES), lambda l, i: (l, i, 0))
    return pl.pallas_call(
        functools.partial(_memory_kv_body, heads=CA_HEADS, tm=tm),
        grid=(layers, m // tm),
        in_specs=[pl.BlockSpec((tm, d), lambda l, i: (i, 0)), pl.BlockSpec((None, 1, d), lambda l, i: (l, 0, 0)),
                  pl.BlockSpec((None, d, 2 * d), lambda l, i: (l, 0, 0))],
        out_specs=[pl.BlockSpec((None, tm, 2 * d), lambda l, i: (l, i, 0)), slab, slab],
        out_shape=[jax.ShapeDtypeStruct((layers, m, 2 * d), BF16),
                   jax.ShapeDtypeStruct((layers, m * rows_per, LANES), F32),
                   jax.ShapeDtypeStruct((layers, m * rows_per, LANES), F32)],
        compiler_params=_cparams("parallel", "parallel"),
        name="memory_kv",
    )(mem, g_all.reshape(layers, 1, d), w_all)


def _decode_t_scores(q_ref, knew_ref, bias_ref, sink_ref, k_ref, b, scale):
    qb = q_ref[b] * scale
    s = _dot(qb.astype(BF16), k_ref[0, b].astype(BF16))
    sn = jnp.sum(qb * knew_ref[b], axis=1, keepdims=True)
    stats = []
    for br in range(bias_ref.shape[0]):
        s_ = s + bias_ref[br]
        m = jnp.maximum(jnp.max(s_, axis=1, keepdims=True), sn)
        p = jnp.exp(s_ - m)
        pn = jnp.exp(sn - m)
        den = jnp.sum(p, axis=1, keepdims=True) + pn
        stats.append((p, pn, den, m + jnp.log(den)))
    if sink_ref is not None:
        facs = [1.0 / ((1.0 + jnp.exp(sink_ref[:, 0:1] - st[3])) * st[2]) for st in stats]
    else:
        top = functools.reduce(jnp.maximum, [st[3] for st in stats])
        es = [jnp.exp(st[3] - top) for st in stats]
        tot = sum(es)
        facs = [e / (tot * st[2]) for e, st in zip(es, stats)]
    p_all = sum(st[0] * f for st, f in zip(stats, facs))
    pn_all = sum(st[1] * f for st, f in zip(stats, facs))
    return p_all.astype(BF16), pn_all


def _decode_t_out(weights, vnew_ref, diag_ref, v_ref, o_ref, b):
    p_all, pn_all = weights
    o = _dot_nt(p_all, v_ref[0, b].astype(BF16)) + pn_all * vnew_ref[b]
    if diag_ref is not None:
        o_ref[b] = jnp.sum(o * diag_ref[...], axis=0, keepdims=True)
    else:
        o_ref[b] = o


def _decode_t_body(*refs, bt, sink, diag, scale):
    it = iter(refs)
    q_ref, knew_ref, vnew_ref, bias_ref = next(it), next(it), next(it), next(it)
    sink_ref = next(it) if sink else None
    diag_ref = next(it) if diag else None
    k_ref, v_ref, o_ref = next(it), next(it), next(it)
    weights = [_decode_t_scores(q_ref, knew_ref, bias_ref, sink_ref, k_ref, b, scale) for b in range(bt)]
    for b in range(bt):
        _decode_t_out(weights[b], vnew_ref, diag_ref, v_ref, o_ref, b)


def _decode_t_specs(q_blk, k_new, v_new, bias, kt_all, vt_all, layer, bt, sink, diag, row_of):
    bs, nh, W = q_blk.shape
    J = kt_all.shape[3]
    in_specs = [pl.BlockSpec((bt, nh, W), lambda *g: (row_of(*g), 0, 0)),
                pl.BlockSpec((bt, 1, W), lambda *g: (row_of(*g), 0, 0)),
                pl.BlockSpec((bt, 1, W), lambda *g: (row_of(*g), 0, 0)), _resident(bias.shape)]
    args = [q_blk, k_new.reshape(bs, 1, W), v_new.reshape(bs, 1, W), bias]
    for extra in (sink, diag):
        if extra is not None:
            in_specs.append(_resident(extra.shape))
            args.append(extra)
    cache_spec = pl.BlockSpec((1, bt, W, J), lambda *g: (layer, row_of(*g), 0, 0))
    in_specs += [cache_spec, cache_spec]
    args += [kt_all, vt_all]
    out_rows = 1 if diag is not None else nh
    out_spec = pl.BlockSpec((bt, out_rows, W), lambda *g: (row_of(*g), 0, 0))
    return in_specs, args, out_spec, jax.ShapeDtypeStruct((bs, out_rows, W), F32)


def _decode_t(q_blk, k_new, v_new, bias, kt_all, vt_all, layer, bt, sink=None, diag=None):
    in_specs, args, out_spec, out_shape = _decode_t_specs(q_blk, k_new, v_new, bias, kt_all, vt_all, layer, bt,
                                                          sink, diag, lambda i: i)
    return pl.pallas_call(
        functools.partial(_decode_t_body, bt=bt, sink=sink is not None, diag=diag is not None,
                          scale=HEAD_DIM ** -0.5),
        grid=(q_blk.shape[0] // bt,),
        in_specs=in_specs,
        out_specs=out_spec,
        out_shape=out_shape,
        compiler_params=_cparams("parallel"),
        name="decode_t",
    )(*args)


def _decode_mem_scores(q_ref, k_ref, b, scale):
    qb = q_ref[b] * scale
    rows = qb.shape[0]
    k3 = k_ref[0, b].reshape(-1, rows, LANES)
    part = jnp.sum(k3 * qb[None], axis=2, keepdims=True)
    s = part + pltpu.roll(part, rows // 2, axis=1)
    m = jnp.max(s, axis=0, keepdims=True)
    p = jnp.exp(s - m)
    return p / jnp.sum(p, axis=0, keepdims=True)


def _decode_mem_out(weights, v_ref, o_ref, b):
    rows = weights.shape[1]
    o_ref[b] = jnp.sum(weights * v_ref[0, b].reshape(-1, rows, LANES), axis=0)


def _decode_mem_specs(q, k_all, v_all, layer, bt, row_of):
    bs, rows, _ = q.shape
    cache_spec = pl.BlockSpec((1, bt, k_all.shape[2], LANES), lambda *g: (layer, row_of(*g), 0, 0))
    q_spec = pl.BlockSpec((bt, rows, LANES), lambda *g: (row_of(*g), 0, 0))
    return [q_spec, cache_spec, cache_spec], [q, k_all, v_all], q_spec, jax.ShapeDtypeStruct(q.shape, F32)


def _conv_sample_body(p_ref, b0_ref, b1_ref, cw_ref, ya_ref, u_ref, *, width):
    bg = p_ref[:, 0:width]
    u = p_ref[:, width:2 * width] * p_ref[:, 2 * width:3 * width]
    conv = cw_ref[0:1, :] * b0_ref[...] + cw_ref[1:2, :] * b1_ref[...] + cw_ref[2:3, :] * u
    ya_ref[...] = (bg * conv).astype(BF16)
    u_ref[...] = u


def _conv_sample(proj, buf0, buf1, cw):
    bs = proj.shape[0]
    width = cw.shape[1]
    full = lambda shape: pl.BlockSpec(shape, lambda i: (0,) * len(shape))
    return pl.pallas_call(
        functools.partial(_conv_sample_body, width=width),
        grid=(1,),
        in_specs=[full((bs, 3 * width)), full((bs, width)), full((bs, width)), full(cw.shape)],
        out_specs=[full((bs, width)), full((bs, width))],
        out_shape=[jax.ShapeDtypeStruct((bs, width), BF16), jax.ShapeDtypeStruct((bs, width), F32)],
        compiler_params=_cparams("arbitrary"),
        name="conv_sample",
    )(proj, buf0, buf1, cw)


def _pool_sample_body(p_ref, buf_ref, pw_ref, sc_ref, y_ref):
    u = p_ref[...]
    outs = []
    for g, w in enumerate(POOL_WINDOWS):
        sl = slice(g * POOL_GC, (g + 1) * POOL_GC)
        ug = u[:, sl]
        acc = ug
        for i in range(1, w):
            acc = acc + buf_ref[POOL_PAD - i][:, sl]
        cnt = float(min(PAST_LEN + 1, w))
        dlt = (acc / cnt - ug).astype(BF16)
        outs.append(_dot(dlt, pw_ref[g]))
    y_ref[...] = (jnp.concatenate(outs, axis=1) * sc_ref[...]).astype(BF16)


def _pool_sample(proj, buf_t, pw, scale):
    bs = proj.shape[0]
    width = scale.shape[0]
    full = lambda shape: pl.BlockSpec(shape, lambda i: (0,) * len(shape))
    return pl.pallas_call(
        _pool_sample_body,
        grid=(1,),
        in_specs=[full((bs, width)), full(buf_t.shape), full(pw.shape), full((1, width))],
        out_specs=full((bs, width)),
        out_shape=jax.ShapeDtypeStruct((bs, width), BF16),
        compiler_params=_cparams("arbitrary"),
        name="pool_sample",
    )(proj, buf_t, pw, scale.reshape(1, width))


def kernel(x_prompt, x_sample, mem_prompt, cache_conv, cache_swa_k, cache_swa_v, state_pool, cache_dil_k, cache_dil_v, cache_mem_k, cache_mem_v, norm_mix, norm_ca, norm_mem, norm_ffn, norm_final, w_in_even, conv_w, sinks, w_out_even, w_in_odd, pool_w, pool_scale, w_out_odd, w_ca_q, w_ca_kv, w_ca_o, w_ffn_in, w_ffn_out):
    B, L, D = x_prompt.shape
    Bs = x_sample.shape[0]
    depth = norm_mix.shape[0]
    mlen = mem_prompt.shape[1]
    A = conv_w.shape[2]
    HQ = D // 2
    n_q = HQ // HEAD_DIM
    n_kv = cache_swa_k.shape[3]
    rep = n_q // n_kv
    KV = n_kv * HEAD_DIM
    dh_ca = D // CA_HEADS
    TM = 512
    TL = 512

    w_in_e, w_out_e, w_in_o, w_out_o = (w.astype(BF16) for w in (w_in_even, w_out_even, w_in_odd, w_out_odd))
    w_q, w_kv, w_o, w_f_in, w_f_out = (w.astype(BF16) for w in (w_ca_q, w_ca_kv, w_ca_o, w_ffn_in, w_ffn_out))
    pool_w_b = pool_w.astype(BF16)

    xp = x_prompt.reshape(B * L, D)
    xs = x_sample.reshape(Bs, D)

    halves = dh_ca // LANES
    slab_rows = CA_HEADS * halves
    to_slab = lambda a: a.reshape(a.shape[:-2] + (CA_HEADS, halves, LANES)).swapaxes(-3, -2)
    from_slab = lambda a: a.swapaxes(-3, -2).reshape(a.shape[:-3] + (CA_HEADS, dh_ca))
    mkv_all, memk_t, memv_t = _memory_kv(mem_prompt.reshape(B * mlen, D), norm_mem, w_kv, TM)
    mkv_all = mkv_all.reshape(depth, B, mlen, 2 * D)
    memk_p = from_slab(memk_t.reshape(depth, B, mlen, halves, CA_HEADS, LANES))
    memv_p = from_slab(memv_t.reshape(depth, B, mlen, halves, CA_HEADS, LANES))
    mem_kc = to_slab(cache_mem_k).reshape(depth, Bs, mlen * slab_rows, LANES)
    mem_vc = to_slab(cache_mem_v).reshape(depth, Bs, mlen * slab_rows, LANES)

    swa_len, dil_len = cache_swa_k.shape[2], cache_dil_k.shape[2]
    to_t = lambda c: jnp.transpose(c, (0, 1, 3, 4, 2)).reshape(c.shape[0], Bs, c.shape[3] * HEAD_DIM, c.shape[2])
    swa_kt, swa_vt, dil_kt, dil_vt = to_t(cache_swa_k), to_t(cache_swa_v), to_t(cache_dil_k), to_t(cache_dil_v)

    pos = jnp.arange(dil_len)
    dil_bias = jnp.stack([jnp.where((pos >= dil_len - w) & ((dil_len - pos) % d == 0), 0.0, NEG_INF)
                          for w, d in DIL_PATTERNS]).astype(F32).reshape(len(DIL_PATTERNS), 1, dil_len)
    swa_bias = jnp.zeros((1, 1, swa_len), F32)
    head_of_row = jnp.arange(n_q)
    dil_diag = jnp.repeat(jnp.eye(n_q, dtype=F32), HEAD_DIM, axis=1)
    swa_diag = jnp.repeat(jax.nn.one_hot(head_of_row // rep, n_kv, dtype=F32), HEAD_DIM, axis=1)

    conv_p, swak_p, swav_p, pool_p, dilk_p, dilv_p = [], [], [], [], [], []
    conv_s, swak_s, swav_s, pool_s, dilk_s, dilv_s = [], [], [], [], [], []
    attn_scale = HEAD_DIM ** -0.5

    def sample_cross_q(xs, l):
        q = _norm_matmul(xs, norm_ca[l], w_q, l, Bs).reshape(Bs, CA_HEADS, dh_ca)
        return to_slab(q).reshape(Bs, slab_rows, LANES)

    def mem_host(q_s, l):
        begin = lambda q, k, v, b: _decode_mem_scores(q, k, b, dh_ca ** -0.5)
        end = lambda weights, q, k, v, out, b: _decode_mem_out(weights, v, out, b)
        return begin, end, lambda bt, row_of: _decode_mem_specs(q_s, mem_kc, mem_vc, l, bt, row_of), Bs

    def sample_ffn(xs, o_s, l):
        o_s = from_slab(o_s.reshape(Bs, halves, CA_HEADS, LANES)).reshape(Bs, D)
        xs = _matmul_residual(xs, [o_s], w_o, l, Bs)
        return _swiglu(xs, norm_ffn[l], w_f_in, w_f_out, l, Bs, norm_final if l == depth - 1 else None)

    def sample_odd_front(xs, l):
        o = l // 2
        q0, k0, v0 = A, A + HQ, A + 2 * HQ
        proj_s = _norm_matmul(xs, norm_mix[l], w_in_o, o, Bs)
        yc_s = _pool_sample(proj_s, state_pool[o].transpose(1, 0, 2), pool_w_b[o], pool_scale[o])
        pool_s.append(jnp.concatenate([state_pool[o][:, 1:], proj_s[:, None, :A]], axis=1))
        k_new, v_new = proj_s[:, k0:k0 + HQ], proj_s[:, v0:v0 + HQ]
        dilk_s.append(k_new.reshape(Bs, 1, n_q, HEAD_DIM))
        dilv_s.append(v_new.reshape(Bs, 1, n_q, HEAD_DIM))
        q_blk = proj_s[:, None, q0:q0 + HQ] * dil_diag[None]
        begin = lambda q, kn, vn, bias, diag, k, v, b: _decode_t_scores(q, kn, bias, None, k, b, attn_scale)
        end = lambda weights, q, kn, vn, bias, diag, k, v, out, b: _decode_t_out(weights, vn, diag, v, out, b)
        specs_fn = lambda bt, row_of: _decode_t_specs(q_blk, k_new, v_new, dil_bias, dil_kt, dil_vt, o, bt, None,
                                                      dil_diag, row_of)
        return yc_s, (begin, end, specs_fn, Bs)

    for l in range(depth):
        g_final = norm_final if l == depth - 1 else None
        if l % 2 == 0:
            e = l // 2
            q0, k0, v0 = 3 * A, 3 * A + HQ, 3 * A + HQ + KV
            ya, qkv, tail = _inproj_even(xp.reshape(B, L, D), norm_mix[l], w_in_e, e, conv_w[e], TM)
            sink_row = jnp.repeat(sinks[e], HEAD_DIM).reshape(1, HQ)
            yb = _band_attn(qkv, 0, HQ // LANES, (HQ + KV) // LANES, ((BLK, 1),), True, sink_row)
            mix_parts, w_mix, mix_layer = [ya, yb], w_out_e, e
            conv_p.append(tail[:, SUBLANES - (CONV_W - 1):])
            swak_p.append(qkv[:, L - BLK:, HQ:HQ + KV].reshape(B, BLK, n_kv, HEAD_DIM))
            swav_p.append(qkv[:, L - BLK:, HQ + KV:].reshape(B, BLK, n_kv, HEAD_DIM))
            proj_s = _norm_matmul(xs, norm_mix[l], w_in_e, e, Bs)
            ya_s, u_s = _conv_sample(proj_s, cache_conv[e][:, 0], cache_conv[e][:, 1], conv_w[e])
            conv_s.append(jnp.stack([cache_conv[e][:, 1], u_s], axis=1))
            k_new, v_new = proj_s[:, k0:k0 + KV], proj_s[:, v0:v0 + KV]
            q_h = proj_s[:, q0:q0 + HQ].reshape(Bs, n_q, 1, HEAD_DIM)
            q_blk = (q_h * swa_diag.reshape(n_q, n_kv, HEAD_DIM)[None]).reshape(Bs, n_q, KV)
            sink_col = jnp.broadcast_to(sinks[e][:, None], (n_q, LANES))
            o_raw = _decode_t(q_blk, k_new, v_new, swa_bias, swa_kt, swa_vt, e, 8, sink=sink_col)
            o_raw = o_raw.reshape(Bs, n_kv, rep, n_kv, HEAD_DIM)
            yb_s = jnp.stack([o_raw[:, g, :, g, :] for g in range(n_kv)], axis=1).reshape(Bs, HQ)
            xs = _matmul_residual(xs, [ya_s, yb_s], w_out_e, e, Bs)
            swak_s.append(k_new.reshape(Bs, 1, n_kv, HEAD_DIM))
            swav_s.append(v_new.reshape(Bs, 1, n_kv, HEAD_DIM))
            cross_host = mem_host(sample_cross_q(xs, l), l)
        else:
            o = l // 2
            q0, k0, v0 = A, A + HQ, A + 2 * HQ
            yc, qkv, kt, vt, ptail = _inproj_odd(xp.reshape(B, L, D), norm_mix[l], w_in_o, o, pool_w_b[o],
                                                 pool_scale[o], TM)
            yd = _band_attn(qkv, 0, HQ // LANES, 2 * HQ // LANES, DIL_PATTERNS, False)
            mix_parts, w_mix, mix_layer = [yc, yd], w_out_o, o
            cross_host = None
            pool_p.append(ptail[:, POOL_CARRY - POOL_PAD:])
            dilk_p.append(kt)
            dilv_p.append(vt)

        xp = _cross_prompt(xp.reshape(B, L, D), mix_parts, w_mix, mix_layer, norm_ca[l], w_q, mkv_all, w_o, l,
                           TL, cross_host)
        if cross_host is not None:
            xp, o_s = xp
            xs = sample_ffn(xs, o_s, l)
        xp = xp.reshape(B * L, D)

        if l % 2 == 0 and l + 1 < depth:
            yc_s, host = sample_odd_front(xs, l + 1)
            xp, yd_s = _swiglu(xp, norm_ffn[l], w_f_in, w_f_out, l, TM, g_final, host)
            xs = _matmul_residual(xs, [yc_s, yd_s.reshape(Bs, HQ)], w_out_o, (l + 1) // 2, Bs)
        elif l % 2 == 1:
            xp, o_s = _swiglu(xp, norm_ffn[l], w_f_in, w_f_out, l, TM, g_final, mem_host(sample_cross_q(xs, l), l))
            xs = sample_ffn(xs, o_s, l)
        else:
            xp = _swiglu(xp, norm_ffn[l], w_f_in, w_f_out, l, TM, g_final)

    from_t = lambda ts: jnp.transpose(jnp.stack(ts).reshape(len(ts), B, n_q, HEAD_DIM, L), (0, 1, 4, 2, 3))
    return (xp.reshape(B, L, D), xs.reshape(Bs, 1, D),
            jnp.stack(conv_p), jnp.stack(swak_p), jnp.stack(swav_p),
            jnp.stack(pool_p), from_t(dilk_p), from_t(dilv_p),
            memk_p, memv_p,
            jnp.stack(conv_s), jnp.stack(swak_s), jnp.stack(swav_s),
            jnp.stack(pool_s), jnp.stack(dilk_s), jnp.stack(dilv_s))
```

```python
import functools

import jax
import jax.numpy as jnp
from jax import lax
from jax.experimental import pallas as pl
from jax.experimental.pallas import tpu as pltpu

F32 = jnp.float32
BF16 = jnp.bfloat16

LANES = 128
SUBLANES = 8
VMEM_LIMIT = 56 * 1024 * 1024

HEAD_DIM = 64
BLK = 128
CONV_W = 3
POOL_WINDOWS = (2, 4, 8, 16)
POOL_PAD = 15
POOL_GC = 128
DIL_PATTERNS = ((128, 1), (512, 4), (2048, 16))
CA_HEADS = 4
PAST_LEN = 8192
RMS_EPS = 1e-6
NEG_INF = -1e30


def _cparams(*sem):
    return pltpu.CompilerParams(dimension_semantics=sem, vmem_limit_bytes=VMEM_LIMIT)


def _rms(x, g):
    ms = jnp.mean(x * x, axis=-1, keepdims=True)
    return x * lax.rsqrt(ms + RMS_EPS) * g


def _dot(a, b):
    return jnp.dot(a, b, preferred_element_type=F32)


def _dot_nt(a, b):
    return lax.dot_general(a, b, (((1,), (1,)), ((), ())), preferred_element_type=F32)


def _resident(shape):
    return pl.BlockSpec(shape, lambda *_: (0,) * len(shape), pipeline_mode=pl.Buffered(1))


def _layer_block(stack, layer, rows=None, row_block=0):
    _, k, n = stack.shape
    return pl.BlockSpec((None, k if rows is None else rows, n), lambda *_: (layer, row_block, 0),
                        pipeline_mode=pl.Buffered(1))


def _norm_matmul_body(x_ref, g_ref, w_ref, o_ref):
    h = _rms(x_ref[...], g_ref[...]).astype(BF16)
    o_ref[...] = _dot(h, w_ref[...])


def _norm_matmul(x, g, w, layer, tm):
    m, d = x.shape
    n = w.shape[2]
    return pl.pallas_call(
        _norm_matmul_body,
        grid=(m // tm,),
        in_specs=[pl.BlockSpec((tm, d), lambda i: (i, 0)), _resident((1, d)), _layer_block(w, layer)],
        out_specs=pl.BlockSpec((tm, n), lambda i: (i, 0)),
        out_shape=jax.ShapeDtypeStruct((m, n), F32),
        compiler_params=_cparams("parallel"),
        name="norm_matmul",
    )(x, g.reshape(1, d), w)


def _matmul_residual_body(*refs, nparts):
    x_ref, o_ref = refs[0], refs[-1]
    acc = x_ref[...]
    for i in range(nparts):
        acc = acc + _dot(refs[1 + 2 * i][...].astype(BF16), refs[2 + 2 * i][...])
    o_ref[...] = acc


def _matmul_residual(x, parts, w, layer, tm):
    m, d = x.shape
    in_specs = [pl.BlockSpec((tm, d), lambda i: (i, 0))]
    args = [x]
    for row_block, a in enumerate(parts):
        in_specs += [pl.BlockSpec((tm, a.shape[1]), lambda i: (i, 0)), _layer_block(w, layer, a.shape[1], row_block)]
        args += [a, w]
    return pl.pallas_call(
        functools.partial(_matmul_residual_body, nparts=len(parts)),
        grid=(m // tm,),
        in_specs=in_specs,
        out_specs=pl.BlockSpec((tm, d), lambda i: (i, 0)),
        out_shape=jax.ShapeDtypeStruct((m, d), F32),
        compiler_params=_cparams("parallel"),
        name="matmul_residual",
    )(*args)


FFN_CHUNK = 256


def _swiglu_body(*refs, hidden, final, nsub, host):
    it = iter(refs)
    x_ref, g_ref, win_ref, wout_ref = next(it), next(it), next(it), next(it)
    gf_ref = next(it) if final else None
    if host is not None:
        host_in = [next(it) for _ in range(host[2])]
    o_ref = next(it)
    host_out = next(it) if host is not None else None
    h_scr, acc_scr = next(it), next(it)
    j = pl.program_id(1)

    @pl.when(j == 0)
    def _():
        x = x_ref[...]
        h_scr[...] = _rms(x, g_ref[...]).astype(BF16)
        acc_scr[...] = x

    nchunk = hidden // FFN_CHUNK
    per = -(-nchunk // nsub)
    for grp in range(nsub):
        @pl.when(j == grp)
        def _(grp=grp):
            h = h_scr[...]
            acc = acc_scr[...]
            first, last = grp * per, min((grp + 1) * per, nchunk)
            begin_at = first + 1 if last - first > 1 else first
            for c in range(first, last):
                if host is not None and c == begin_at:
                    state = host[0](*host_in, 0)
                lo = c * FFN_CHUNK
                gate = _dot(h, win_ref[:, lo:lo + FFN_CHUNK])
                up = _dot(h, win_ref[:, hidden + lo:hidden + lo + FFN_CHUNK])
                act = (gate * (1.0 / (1.0 + jnp.exp(-gate))) * up).astype(BF16)
                acc = acc + _dot(act, wout_ref[lo:lo + FFN_CHUNK, :])
            if host is not None:
                host[1](state, *host_in, host_out, 0)
            if grp < nsub - 1:
                acc_scr[...] = acc
            else:
                o_ref[...] = _rms(acc, gf_ref[...]) if final else acc


def _swiglu(x, g, win, wout, layer, tm, g_final=None, host=None):
    m, d = x.shape
    hidden = wout.shape[1]
    final = g_final is not None
    steps = m // tm
    in_specs = [pl.BlockSpec((tm, d), lambda i, j: (i, 0)), _resident((1, d)), _layer_block(win, layer),
                _layer_block(wout, layer)]
    args = [x, g.reshape(1, d), win, wout]
    if final:
        in_specs.append(_resident((1, d)))
        args.append(g_final.reshape(1, d))
    out_specs = [pl.BlockSpec((tm, d), lambda i, j: (i, 0))]
    out_shape = [jax.ShapeDtypeStruct((m, d), F32)]
    nsub, host_arg = 1, None
    if host is not None:
        begin_fn, end_fn, specs_fn, bs = host
        nsub = bs // steps
        h_specs, h_args, h_out_spec, h_out_shape = specs_fn(1, lambda i, j: i * nsub + j)
        in_specs += h_specs
        args += h_args
        out_specs.append(h_out_spec)
        out_shape.append(h_out_shape)
        host_arg = (begin_fn, end_fn, len(h_specs))
    res = pl.pallas_call(
        functools.partial(_swiglu_body, hidden=hidden, final=final, nsub=nsub, host=host_arg),
        grid=(steps, nsub),
        in_specs=in_specs,
        out_specs=out_specs,
        out_shape=out_shape,
        scratch_shapes=[pltpu.VMEM((tm, d), BF16), pltpu.VMEM((tm, d), F32)],
        compiler_params=_cparams("parallel", "arbitrary"),
        name="swiglu",
    )(*args)
    return res if host is not None else res[0]


def _inproj_even_body(x_ref, g_ref, w_ref, cw_ref, ya_ref, qkv_ref, tail_ref, ubuf, *, tm, width):
    @pl.when(pl.program_id(1) == 0)
    def _():
        ubuf[0:SUBLANES, :] = jnp.zeros((SUBLANES, width), F32)

    h = _rms(x_ref[0], g_ref[...]).astype(BF16)
    r = _dot(h, w_ref[...])
    qkv_ref[0] = r[:, 3 * width:]
    u = r[:, width:2 * width] * r[:, 2 * width:3 * width]
    ubuf[SUBLANES:SUBLANES + tm, :] = u
    conv = (cw_ref[0:1, :] * ubuf[SUBLANES - 2:SUBLANES - 2 + tm, :]
            + cw_ref[1:2, :] * ubuf[SUBLANES - 1:SUBLANES - 1 + tm, :]
            + cw_ref[2:3, :] * u)
    ya_ref[0] = (r[:, 0:width] * conv).astype(BF16)
    tail = ubuf[tm:tm + SUBLANES, :]
    tail_ref[0] = tail
    ubuf[0:SUBLANES, :] = tail


def _inproj_even(x, g, w, layer, cw, tm):
    b, l, d = x.shape
    n = w.shape[2]
    width = cw.shape[1]
    rest = n - 3 * width
    return pl.pallas_call(
        functools.partial(_inproj_even_body, tm=tm, width=width),
        grid=(b, l // tm),
        in_specs=[pl.BlockSpec((1, tm, d), lambda i, j: (i, j, 0)), _resident((1, d)), _layer_block(w, layer),
                  _resident(cw.shape)],
        out_specs=[pl.BlockSpec((1, tm, width), lambda i, j: (i, j, 0)),
                   pl.BlockSpec((1, tm, rest), lambda i, j: (i, j, 0)),
                   pl.BlockSpec((1, SUBLANES, width), lambda i, j: (i, 0, 0))],
        out_shape=[jax.ShapeDtypeStruct((b, l, width), BF16), jax.ShapeDtypeStruct((b, l, rest), F32),
                   jax.ShapeDtypeStruct((b, SUBLANES, width), F32)],
        scratch_shapes=[pltpu.VMEM((tm + SUBLANES, width), F32)],
        compiler_params=_cparams("parallel", "arbitrary"),
        name="inproj_even",
    )(x, g.reshape(1, d), w, cw)


POOL_CARRY = 16


def _inproj_odd_body(x_ref, g_ref, w_ref, pw_ref, sc_ref, yc_ref, qkv_ref, kt_ref, vt_ref, tail_ref, ubuf, *, tm, width):
    @pl.when(pl.program_id(1) == 0)
    def _():
        ubuf[0:POOL_CARRY, :] = jnp.zeros((POOL_CARRY, width), F32)

    h = _rms(x_ref[0], g_ref[...]).astype(BF16)
    r = _dot(h, w_ref[...])
    qkv_ref[0] = r[:, width:]
    kt_ref[0] = r[:, 2 * width:3 * width].T
    vt_ref[0] = r[:, 3 * width:4 * width].T
    u = r[:, 0:width]
    ubuf[POOL_CARRY:POOL_CARRY + tm, :] = u
    pos = pl.program_id(1) * tm + lax.broadcasted_iota(jnp.int32, (tm, 1), 0)
    outs = []
    for g, w in enumerate(POOL_WINDOWS):
        sl = slice(g * POOL_GC, (g + 1) * POOL_GC)
        ug = u[:, sl]
        acc = ug
        for i in range(1, w):
            acc = acc + ubuf[POOL_CARRY - i:POOL_CARRY - i + tm, sl]
        cnt = jnp.minimum(pos + 1, w).astype(F32)
        dlt = (acc / cnt - ug).astype(BF16)
        outs.append(_dot(dlt, pw_ref[g]))
    yc_ref[0] = (jnp.concatenate(outs, axis=1) * sc_ref[...]).astype(BF16)
    tail = ubuf[tm:tm + POOL_CARRY, :]
    tail_ref[0] = tail
    ubuf[0:POOL_CARRY, :] = tail


def _inproj_odd(x, g, w, layer, pw, scale, tm):
    b, l, d = x.shape
    n = w.shape[2]
    width = scale.shape[0]
    t_spec = pl.BlockSpec((1, width, tm), lambda i, j: (i, 0, j))
    return pl.pallas_call(
        functools.partial(_inproj_odd_body, tm=tm, width=width),
        grid=(b, l // tm),
        in_specs=[pl.BlockSpec((1, tm, d), lambda i, j: (i, j, 0)), _resident((1, d)), _layer_block(w, layer),
                  _resident(pw.shape), _resident((1, width))],
        out_specs=[pl.BlockSpec((1, tm, width), lambda i, j: (i, j, 0)),
                   pl.BlockSpec((1, tm, n - width), lambda i, j: (i, j, 0)), t_spec, t_spec,
                   pl.BlockSpec((1, POOL_CARRY, width), lambda i, j: (i, 0, 0))],
        out_shape=[jax.ShapeDtypeStruct((b, l, width), BF16), jax.ShapeDtypeStruct((b, l, n - width), F32),
                   jax.ShapeDtypeStruct((b, width, l), F32), jax.ShapeDtypeStruct((b, width, l), F32),
                   jax.ShapeDtypeStruct((b, POOL_CARRY, width), F32)],
        scratch_shapes=[pltpu.VMEM((tm + POOL_CARRY, width), F32)],
        compiler_params=_cparams("parallel", "arbitrary"),
        name="inproj_odd",
    )(x, g.reshape(1, d), w, pw, scale.reshape(1, width))


TILES_IN_FLIGHT = 16


def _unroll(trips, limit=TILES_IN_FLIGHT):
    return max(u for u in range(1, max(limit, 1) + 1) if trips % u == 0)


def _band_attn_body(*refs, branches, gqa, sink, seq):
    nbr = len(branches)
    if sink:
        q_ref, k_ref, v_ref, sink_ref, y_ref = refs[:5]
        scr = refs[5:]
    else:
        q_ref, k_ref, v_ref, y_ref = refs[:4]
        scr = refs[4:]
    o_scr, l_scr = scr[:nbr], scr[nbr:2 * nbr]
    bias_first, bias_band = scr[2 * nbr], scr[2 * nbr + 1]
    lane = lax.broadcasted_iota(jnp.int32, (1, LANES), 1)
    lo = lane < HEAD_DIM

    rr = lax.broadcasted_iota(jnp.int32, (2 * BLK, 2 * BLK), 0) & (BLK - 1)
    cc = lax.broadcasted_iota(jnp.int32, (2 * BLK, 2 * BLK), 1)
    bias_band[...] = jnp.where((cc >= rr) & (cc <= rr + BLK), 0.0, NEG_INF).astype(F32)
    rr1 = lax.broadcasted_iota(jnp.int32, (2 * BLK, BLK), 0) & (BLK - 1)
    cc1 = lax.broadcasted_iota(jnp.int32, (2 * BLK, BLK), 1)
    bias_first[...] = jnp.where(cc1 <= rr1, 0.0, NEG_INF).astype(F32)

    if gqa:
        kx, vx = scr[2 * nbr + 2], scr[2 * nbr + 3]
        keep = (lane // HEAD_DIM) == (pl.program_id(1) // 2)
        chunk = 2 * BLK
        for c in range(seq // chunk):
            rows = slice(c * chunk, (c + 1) * chunk)
            kb = k_ref[0, rows, :]
            vb = v_ref[0, rows, :]
            kx[rows, :] = jnp.where(keep, kb, pltpu.roll(kb, HEAD_DIM, axis=1))
            vx[rows, :] = jnp.where(keep, vb, pltpu.roll(vb, HEAD_DIM, axis=1))
        load_k = lambda idx: kx[idx, :]
        load_v = lambda idx: vx[idx, :]
    else:
        load_k = lambda idx: k_ref[0, idx, :]
        load_v = lambda idx: v_ref[0, idx, :]
    load_q = lambda idx: q_ref[0, idx, :]

    scale = HEAD_DIM ** -0.5

    def tile(q_t, k_t, v_t, bias):
        qs = q_t * scale
        qm = jnp.concatenate([jnp.where(lo, qs, 0.0), jnp.where(lo, 0.0, qs)], axis=0).astype(BF16)
        s = _dot_nt(qm, k_t.astype(BF16)) + bias
        m = jnp.max(s, axis=1, keepdims=True)
        p = jnp.exp(s - m)
        den = jnp.sum(p, axis=1, keepdims=True)
        o2 = _dot(p.astype(BF16), v_t.astype(BF16)) / den
        lse = jnp.broadcast_to(m + jnp.log(den), (2 * BLK, LANES))
        o = jnp.where(lo, o2[:BLK], o2[BLK:])
        return o, jnp.where(lo, lse[:BLK], lse[BLK:])

    for bi, (w, d) in enumerate(branches):
        nblk = seq // (d * BLK)

        def rows(start, size, d=d):
            if isinstance(start, int):
                return pl.ds(start, size) if d == 1 else pl.ds(start, size, stride=d)
            if d == 1:
                return pl.ds(pl.multiple_of(start, BLK), size)
            return pl.ds(start, size, stride=d)

        def run_class(r, bi=bi, d=d, nblk=nblk, rows=rows):
            idx0 = rows(r, BLK)
            o, lse = tile(load_q(idx0), load_k(idx0), load_v(idx0), bias_first[...])
            o_scr[bi][idx0, :] = o
            l_scr[bi][idx0, :] = lse

            def body(i, carry):
                qi = rows(r + d * BLK * i, BLK)
                ki = rows(r + d * BLK * (i - 1), 2 * BLK)
                o, lse = tile(load_q(qi), load_k(ki), load_v(ki), bias_band[...])
                o_scr[bi][qi, :] = o
                l_scr[bi][qi, :] = lse
                return carry

            if nblk > 1:
                lax.fori_loop(1, nblk, body, 0, unroll=_unroll(nblk - 1))

        if d == 1:
            run_class(0)
        else:
            lax.fori_loop(0, d, lambda r, c, run_class=run_class: (run_class(r), c)[1], 0,
                          unroll=_unroll(d, TILES_IN_FLIGHT // min(nblk, TILES_IN_FLIGHT)))

    chunk = 2 * BLK
    for c in range(seq // chunk):
        rows_c = slice(c * chunk, (c + 1) * chunk)
        if sink:
            lse = l_scr[0][rows_c, :]
            gate = 1.0 / (1.0 + jnp.exp(sink_ref[...] - lse))
            y = o_scr[0][rows_c, :] * gate
        else:
            lses = [l_scr[bi][rows_c, :] for bi in range(nbr)]
            top = functools.reduce(jnp.maximum, lses)
            es = [jnp.exp(x - top) for x in lses]
            y = sum(e * o_scr[bi][rows_c, :] for bi, e in enumerate(es)) / sum(es)
        y_ref[0, rows_c, :] = y.astype(BF16)


def _band_attn(proj, q_blk, k_blk, v_blk, branches, gqa, sink_row=None):
    b, seq, _ = proj.shape
    nlb = 4
    nbr = len(branches)
    sink = sink_row is not None
    blk = (1, seq, LANES)
    kv_map = (lambda off: (lambda i, j: (i, 0, off))) if gqa else (lambda off: (lambda i, j: (i, 0, off + j)))
    in_specs = [pl.BlockSpec(blk, lambda i, j: (i, 0, q_blk + j)), pl.BlockSpec(blk, kv_map(k_blk)),
                pl.BlockSpec(blk, kv_map(v_blk))]
    args = [proj, proj, proj]
    if sink:
        in_specs.append(pl.BlockSpec((1, LANES), lambda i, j: (0, j)))
        args.append(sink_row)
    scratch = [pltpu.VMEM((seq, LANES), F32) for _ in range(2 * nbr)]
    scratch += [pltpu.VMEM((2 * BLK, BLK), F32), pltpu.VMEM((2 * BLK, 2 * BLK), F32)]
    if gqa:
        scratch += [pltpu.VMEM((seq, LANES), F32), pltpu.VMEM((seq, LANES), F32)]
    return pl.pallas_call(
        functools.partial(_band_attn_body, branches=branches, gqa=gqa, sink=sink, seq=seq),
        grid=(b, nlb),
        in_specs=in_specs,
        out_specs=pl.BlockSpec(blk, lambda i, j: (i, 0, j)),
        out_shape=jax.ShapeDtypeStruct((b, seq, nlb * LANES), BF16),
        scratch_shapes=scratch,
        compiler_params=_cparams("parallel", "parallel"),
        name="band_attn",
    )(*args)


def _cross_prompt_body(*refs, heads, nparts, host):
    x_ref = refs[0]
    g_ref, wq_ref, mkv_ref, wo_ref = refs[1 + 2 * nparts:5 + 2 * nparts]
    rest = refs[5 + 2 * nparts:]
    if host is not None:
        begin_fn, end_fn, n_in, bt = host
        host_in, o_ref, host_out = rest[:n_in], rest[n_in], rest[n_in + 1]
        states = [begin_fn(*host_in, b) for b in range(bt)]
    else:
        o_ref = rest[0]
    x = x_ref[0]
    for i in range(nparts):
        x = x + _dot(refs[1 + 2 * i][0], refs[2 + 2 * i][...])
    d = x.shape[1]
    dh = d // heads
    h = _rms(x, g_ref[...]).astype(BF16)
    q = (_dot(h, wq_ref[...]) * (dh ** -0.5)).astype(BF16)
    outs = []
    for c in range(heads):
        kc = mkv_ref[0, :, c * dh:(c + 1) * dh]
        vc = mkv_ref[0, :, d + c * dh:d + (c + 1) * dh]
        s = _dot_nt(q[:, c * dh:(c + 1) * dh], kc)
        m = jnp.max(s, axis=1, keepdims=True)
        p = jnp.exp(s - m)
        den = jnp.sum(p, axis=1, keepdims=True)
        outs.append((_dot(p.astype(BF16), vc) / den).astype(BF16))
    o_ref[0] = x + _dot(jnp.concatenate(outs, axis=1), wo_ref[...])
    if host is not None:
        for b in range(bt):
            end_fn(states[b], *host_in, host_out, b)


def _cross_prompt(x, parts, w_mix, mix_layer, g, wq, mkv_all, wo, layer, tl, host=None):
    b, l, d = x.shape
    mlen = mkv_all.shape[2]
    in_specs = [pl.BlockSpec((1, tl, d), lambda i, j: (i, j, 0))]
    args = [x]
    for row_block, a in enumerate(parts):
        in_specs += [pl.BlockSpec((1, tl, a.shape[2]), lambda i, j: (i, j, 0)),
                     _layer_block(w_mix, mix_layer, a.shape[2], row_block)]
        args += [a, w_mix]
    in_specs += [_resident((1, d)), _layer_block(wq, layer),
                 pl.BlockSpec((None, 1, mlen, 2 * d), lambda i, j: (layer, i, 0, 0)), _layer_block(wo, layer)]
    args += [g.reshape(1, d), wq, mkv_all, wo]
    out_specs = [pl.BlockSpec((1, tl, d), lambda i, j: (i, j, 0))]
    out_shape = [jax.ShapeDtypeStruct((b, l, d), F32)]
    host_arg = None
    if host is not None:
        begin_fn, end_fn, specs_fn, rows = host
        inner = l // tl
        bt = rows // (b * inner)
        h_specs, h_args, h_out_spec, h_out_shape = specs_fn(bt, lambda i, j: i * inner + j)
        in_specs += h_specs
        args += h_args
        out_specs.append(h_out_spec)
        out_shape.append(h_out_shape)
        host_arg = (begin_fn, end_fn, len(h_specs), bt)
    res = pl.pallas_call(
        functools.partial(_cross_prompt_body, heads=CA_HEADS, nparts=len(parts), host=host_arg),
        grid=(b, l // tl),
        in_specs=in_specs,
        out_specs=out_specs,
        out_shape=out_shape,
        compiler_params=_cparams("parallel", "parallel"),
        name="cross_prompt",
    )(*args)
    return res if host is not None else res[0]


def _memory_kv_body(x_ref, g_ref, w_ref, mkv_ref, kt_ref, vt_ref, *, heads, tm):
    h = _rms(x_ref[...], g_ref[...]).astype(BF16)
    r = _dot(h, w_ref[...])
    mkv_ref[...] = r.astype(BF16)
    d = r.shape[1] // 2
    halves = d // heads // LANES
    for t, out_ref in enumerate((kt_ref, vt_ref)):
        for hd in range(heads):
            for c in range(halves):
                col = t * d + hd * (d // heads) + c * LANES
                out_ref[pl.ds(c * heads + hd, tm, stride=heads * halves), :] = r[:, col:col + LANES]


def _memory_kv(mem, g_all, w_all, tm):
    m, d = mem.shape
    layers = w_all.shape[0]
    rows_per = d // LANES
    slab = pl.BlockSpec((None, tm * rows_per, LANES), lambda l, i: (l, i, 0))
    return pl.pallas_call(
        functools.partial(_memory_kv_body, heads=CA_HEADS, tm=tm),
        grid=(layers, m // tm),
        in_specs=[pl.BlockSpec((tm, d), lambda l, i: (i, 0)), pl.BlockSpec((None, 1, d), lambda l, i: (l, 0, 0)),
                  pl.BlockSpec((None, d, 2 * d), lambda l, i: (l, 0, 0))],
        out_specs=[pl.BlockSpec((None, tm, 2 * d), lambda l, i: (l, i, 0)), slab, slab],
        out_shape=[jax.ShapeDtypeStruct((layers, m, 2 * d), BF16),
                   jax.ShapeDtypeStruct((layers, m * rows_per, LANES), F32),
                   jax.ShapeDtypeStruct((layers, m * rows_per, LANES), F32)],
        compiler_params=_cparams("parallel", "parallel"),
        name="memory_kv",
    )(mem, g_all.reshape(layers, 1, d), w_all)


def _decode_t_scores(q_ref, knew_ref, bias_ref, sink_ref, k_ref, b, scale):
    qb = q_ref[b] * scale
    s = _dot(qb.astype(BF16), k_ref[0, b].astype(BF16))
    sn = jnp.sum(qb * knew_ref[b], axis=1, keepdims=True)
    stats = []
    for br in range(bias_ref.shape[0]):
        s_ = s + bias_ref[br]
        m = jnp.maximum(jnp.max(s_, axis=1, keepdims=True), sn)
        p = jnp.exp(s_ - m)
        pn = jnp.exp(sn - m)
        den = jnp.sum(p, axis=1, keepdims=True) + pn
        stats.append((p, pn, den, m + jnp.log(den)))
    if sink_ref is not None:
        facs = [1.0 / ((1.0 + jnp.exp(sink_ref[:, 0:1] - st[3])) * st[2]) for st in stats]
    else:
        top = functools.reduce(jnp.maximum, [st[3] for st in stats])
        es = [jnp.exp(st[3] - top) for st in stats]
        tot = sum(es)
        facs = [e / (tot * st[2]) for e, st in zip(es, stats)]
    p_all = sum(st[0] * f for st, f in zip(stats, facs))
    pn_all = sum(st[1] * f for st, f in zip(stats, facs))
    return p_all.astype(BF16), pn_all


def _decode_t_out(weights, vnew_ref, diag_ref, v_ref, o_ref, b):
    p_all, pn_all = weights
    o = _dot_nt(p_all, v_ref[0, b].astype(BF16)) + pn_all * vnew_ref[b]
    if diag_ref is not None:
        o_ref[b] = jnp.sum(o * diag_ref[...], axis=0, keepdims=True)
    else:
        o_ref[b] = o


def _decode_t_body(*refs, bt, sink, diag, scale):
    it = iter(refs)
    q_ref, knew_ref, vnew_ref, bias_ref = next(it), next(it), next(it), next(it)
    sink_ref = next(it) if sink else None
    diag_ref = next(it) if diag else None
    k_ref, v_ref, o_ref = next(it), next(it), next(it)
    weights = [_decode_t_scores(q_ref, knew_ref, bias_ref, sink_ref, k_ref, b, scale) for b in range(bt)]
    for b in range(bt):
        _decode_t_out(weights[b], vnew_ref, diag_ref, v_ref, o_ref, b)


def _decode_t_specs(q_blk, k_new, v_new, bias, kt_all, vt_all, layer, bt, sink, diag, row_of):
    bs, nh, W = q_blk.shape
    J = kt_all.shape[3]
    in_specs = [pl.BlockSpec((bt, nh, W), lambda *g: (row_of(*g), 0, 0)),
                pl.BlockSpec((bt, 1, W), lambda *g: (row_of(*g), 0, 0)),
                pl.BlockSpec((bt, 1, W), lambda *g: (row_of(*g), 0, 0)), _resident(bias.shape)]
    args = [q_blk, k_new.reshape(bs, 1, W), v_new.reshape(bs, 1, W), bias]
    for extra in (sink, diag):
        if extra is not None:
            in_specs.append(_resident(extra.shape))
            args.append(extra)
    cache_spec = pl.BlockSpec((1, bt, W, J), lambda *g: (layer, row_of(*g), 0, 0))
    in_specs += [cache_spec, cache_spec]
    args += [kt_all, vt_all]
    out_rows = 1 if diag is not None else nh
    out_spec = pl.BlockSpec((bt, out_rows, W), lambda *g: (row_of(*g), 0, 0))
    return in_specs, args, out_spec, jax.ShapeDtypeStruct((bs, out_rows, W), F32)


def _decode_t(q_blk, k_new, v_new, bias, kt_all, vt_all, layer, bt, sink=None, diag=None):
    in_specs, args, out_spec, out_shape = _decode_t_specs(q_blk, k_new, v_new, bias, kt_all, vt_all, layer, bt,
                                                          sink, diag, lambda i: i)
    return pl.pallas_call(
        functools.partial(_decode_t_body, bt=bt, sink=sink is not None, diag=diag is not None,
                          scale=HEAD_DIM ** -0.5),
        grid=(q_blk.shape[0] // bt,),
        in_specs=in_specs,
        out_specs=out_spec,
        out_shape=out_shape,
        compiler_params=_cparams("parallel"),
        name="decode_t",
    )(*args)


def _decode_mem_scores(q_ref, k_ref, b, scale):
    qb = q_ref[b] * scale
    rows = qb.shape[0]
    k3 = k_ref[0, b].reshape(-1, rows, LANES)
    part = jnp.sum(k3 * qb[None], axis=2, keepdims=True)
    s = part + pltpu.roll(part, rows // 2, axis=1)
    m = jnp.max(s, axis=0, keepdims=True)
    p = jnp.exp(s - m)
    return p / jnp.sum(p, axis=0, keepdims=True)


def _decode_mem_out(weights, v_ref, o_ref, b):
    rows = weights.shape[1]
    o_ref[b] = jnp.sum(weights * v_ref[0, b].reshape(-1, rows, LANES), axis=0)


def _decode_mem_specs(q, k_all, v_all, layer, bt, row_of):
    bs, rows, _ = q.shape
    cache_spec = pl.BlockSpec((1, bt, k_all.shape[2], LANES), lambda *g: (layer, row_of(*g), 0, 0))
    q_spec = pl.BlockSpec((bt, rows, LANES), lambda *g: (row_of(*g), 0, 0))
    return [q_spec, cache_spec, cache_spec], [q, k_all, v_all], q_spec, jax.ShapeDtypeStruct(q.shape, F32)


def _conv_sample_body(p_ref, b0_ref, b1_ref, cw_ref, ya_ref, u_ref, *, width):
    bg = p_ref[:, 0:width]
    u = p_ref[:, width:2 * width] * p_ref[:, 2 * width:3 * width]
    conv = cw_ref[0:1, :] * b0_ref[...] + cw_ref[1:2, :] * b1_ref[...] + cw_ref[2:3, :] * u
    ya_ref[...] = (bg * conv).astype(BF16)
    u_ref[...] = u


def _conv_sample(proj, buf0, buf1, cw):
    bs = proj.shape[0]
    width = cw.shape[1]
    full = lambda shape: pl.BlockSpec(shape, lambda i: (0,) * len(shape))
    return pl.pallas_call(
        functools.partial(_conv_sample_body, width=width),
        grid=(1,),
        in_specs=[full((bs, 3 * width)), full((bs, width)), full((bs, width)), full(cw.shape)],
        out_specs=[full((bs, width)), full((bs, width))],
        out_shape=[jax.ShapeDtypeStruct((bs, width), BF16), jax.ShapeDtypeStruct((bs, width), F32)],
        compiler_params=_cparams("arbitrary"),
        name="conv_sample",
    )(proj, buf0, buf1, cw)


def _pool_sample_body(p_ref, buf_ref, pw_ref, sc_ref, y_ref):
    u = p_ref[...]
    outs = []
    for g, w in enumerate(POOL_WINDOWS):
        sl = slice(g * POOL_GC, (g + 1) * POOL_GC)
        ug = u[:, sl]
        acc = ug
        for i in range(1, w):
            acc = acc + buf_ref[POOL_PAD - i][:, sl]
        cnt = float(min(PAST_LEN + 1, w))
        dlt = (acc / cnt - ug).astype(BF16)
        outs.append(_dot(dlt, pw_ref[g]))
    y_ref[...] = (jnp.concatenate(outs, axis=1) * sc_ref[...]).astype(BF16)


def _pool_sample(proj, buf_t, pw, scale):
    bs = proj.shape[0]
    width = scale.shape[0]
    full = lambda shape: pl.BlockSpec(shape, lambda i: (0,) * len(shape))
    return pl.pallas_call(
        _pool_sample_body,
        grid=(1,),
        in_specs=[full((bs, width)), full(buf_t.shape), full(pw.shape), full((1, width))],
        out_specs=full((bs, width)),
        out_shape=jax.ShapeDtypeStruct((bs, width), BF16),
        compiler_params=_cparams("arbitrary"),
        name="pool_sample",
    )(proj, buf_t, pw, scale.reshape(1, width))


def kernel(x_prompt, x_sample, mem_prompt, cache_conv, cache_swa_k, cache_swa_v, state_pool, cache_dil_k, cache_dil_v, cache_mem_k, cache_mem_v, norm_mix, norm_ca, norm_mem, norm_ffn, norm_final, w_in_even, conv_w, sinks, w_out_even, w_in_odd, pool_w, pool_scale, w_out_odd, w_ca_q, w_ca_kv, w_ca_o, w_ffn_in, w_ffn_out):
    B, L, D = x_prompt.shape
    Bs = x_sample.shape[0]
    depth = norm_mix.shape[0]
    mlen = mem_prompt.shape[1]
    A = conv_w.shape[2]
    HQ = D // 2
    n_q = HQ // HEAD_DIM
    n_kv = cache_swa_k.shape[3]
    rep = n_q // n_kv
    KV = n_kv * HEAD_DIM
    dh_ca = D // CA_HEADS
    TM = 512
    TL = 512

    w_in_e, w_out_e, w_in_o, w_out_o = (w.astype(BF16) for w in (w_in_even, w_out_even, w_in_odd, w_out_odd))
    w_q, w_kv, w_o, w_f_in, w_f_out = (w.astype(BF16) for w in (w_ca_q, w_ca_kv, w_ca_o, w_ffn_in, w_ffn_out))
    pool_w_b = pool_w.astype(BF16)

    xp = x_prompt.reshape(B * L, D)
    xs = x_sample.reshape(Bs, D)

    halves = dh_ca // LANES
    slab_rows = CA_HEADS * halves
    to_slab = lambda a: a.reshape(a.shape[:-2] + (CA_HEADS, halves, LANES)).swapaxes(-3, -2)
    from_slab = lambda a: a.swapaxes(-3, -2).reshape(a.shape[:-3] + (CA_HEADS, dh_ca))
    mkv_all, memk_t, memv_t = _memory_kv(mem_prompt.reshape(B * mlen, D), norm_mem, w_kv, TM)
    mkv_all = mkv_all.reshape(depth, B, mlen, 2 * D)
    memk_p = from_slab(memk_t.reshape(depth, B, mlen, halves, CA_HEADS, LANES))
    memv_p = from_slab(memv_t.reshape(depth, B, mlen, halves, CA_HEADS, LANES))
    mem_kc = to_slab(cache_mem_k).reshape(depth, Bs, mlen * slab_rows, LANES)
    mem_vc = to_slab(cache_mem_v).reshape(depth, Bs, mlen * slab_rows, LANES)

    swa_len, dil_len = cache_swa_k.shape[2], cache_dil_k.shape[2]
    to_t = lambda c: jnp.transpose(c, (0, 1, 3, 4, 2)).reshape(c.shape[0], Bs, c.shape[3] * HEAD_DIM, c.shape[2])
    swa_kt, swa_vt, dil_kt, dil_vt = to_t(cache_swa_k), to_t(cache_swa_v), to_t(cache_dil_k), to_t(cache_dil_v)

    pos = jnp.arange(dil_len)
    dil_bias = jnp.stack([jnp.where((pos >= dil_len - w) & ((dil_len - pos) % d == 0), 0.0, NEG_INF)
                          for w, d in DIL_PATTERNS]).astype(F32).reshape(len(DIL_PATTERNS), 1, dil_len)
    swa_bias = jnp.zeros((1, 1, swa_len), F32)
    head_of_row = jnp.arange(n_q)
    dil_diag = jnp.repeat(jnp.eye(n_q, dtype=F32), HEAD_DIM, axis=1)
    swa_diag = jnp.repeat(jax.nn.one_hot(head_of_row // rep, n_kv, dtype=F32), HEAD_DIM, axis=1)

    conv_p, swak_p, swav_p, pool_p, dilk_p, dilv_p = [], [], [], [], [], []
    conv_s, swak_s, swav_s, pool_s, dilk_s, dilv_s = [], [], [], [], [], []
    attn_scale = HEAD_DIM ** -0.5

    def sample_cross_q(xs, l):
        q = _norm_matmul(xs, norm_ca[l], w_q, l, Bs).reshape(Bs, CA_HEADS, dh_ca)
        return to_slab(q).reshape(Bs, slab_rows, LANES)

    def mem_host(q_s, l):
        begin = lambda q, k, v, b: _decode_mem_scores(q, k, b, dh_ca ** -0.5)
        end = lambda weights, q, k, v, out, b: _decode_mem_out(weights, v, out, b)
        return begin, end, lambda bt, row_of: _decode_mem_specs(q_s, mem_kc, mem_vc, l, bt, row_of), Bs

    def sample_ffn(xs, o_s, l):
        o_s = from_slab(o_s.reshape(Bs, halves, CA_HEADS, LANES)).reshape(Bs, D)
        xs = _matmul_residual(xs, [o_s], w_o, l, Bs)
        return _swiglu(xs, norm_ffn[l], w_f_in, w_f_out, l, Bs, norm_final if l == depth - 1 else None)

    def sample_odd_front(xs, l):
        o = l // 2
        q0, k0, v0 = A, A + HQ, A + 2 * HQ
        proj_s = _norm_matmul(xs, norm_mix[l], w_in_o, o, Bs)
        yc_s = _pool_sample(proj_s, state_pool[o].transpose(1, 0, 2), pool_w_b[o], pool_scale[o])
        pool_s.append(jnp.concatenate([state_pool[o][:, 1:], proj_s[:, None, :A]], axis=1))
        k_new, v_new = proj_s[:, k0:k0 + HQ], proj_s[:, v0:v0 + HQ]
        dilk_s.append(k_new.reshape(Bs, 1, n_q, HEAD_DIM))
        dilv_s.append(v_new.reshape(Bs, 1, n_q, HEAD_DIM))
        q_blk = proj_s[:, None, q0:q0 + HQ] * dil_diag[None]
        begin = lambda q, kn, vn, bias, diag, k, v, b: _decode_t_scores(q, kn, bias, None, k, b, attn_scale)
        end = lambda weights, q, kn, vn, bias, diag, k, v, out, b: _decode_t_out(weights, vn, diag, v, out, b)
        specs_fn = lambda bt, row_of: _decode_t_specs(q_blk, k_new, v_new, dil_bias, dil_kt, dil_vt, o, bt, None,
                                                      dil_diag, row_of)
        return yc_s, (begin, end, specs_fn, Bs)

    for l in range(depth):
        g_final = norm_final if l == depth - 1 else None
        if l % 2 == 0:
            e = l // 2
            q0, k0, v0 = 3 * A, 3 * A + HQ, 3 * A + HQ + KV
            ya, qkv, tail = _inproj_even(xp.reshape(B, L, D), norm_mix[l], w_in_e, e, conv_w[e], TM)
            sink_row = jnp.repeat(sinks[e], HEAD_DIM).reshape(1, HQ)
            yb = _band_attn(qkv, 0, HQ // LANES, (HQ + KV) // LANES, ((BLK, 1),), True, sink_row)
            mix_parts, w_mix, mix_layer = [ya, yb], w_out_e, e
            conv_p.append(tail[:, SUBLANES - (CONV_W - 1):])
            swak_p.append(qkv[:, L - BLK:, HQ:HQ + KV].reshape(B, BLK, n_kv, HEAD_DIM))
            swav_p.append(qkv[:, L - BLK:, HQ + KV:].reshape(B, BLK, n_kv, HEAD_DIM))
            proj_s = _norm_matmul(xs, norm_mix[l], w_in_e, e, Bs)
            ya_s, u_s = _conv_sample(proj_s, cache_conv[e][:, 0], cache_conv[e][:, 1], conv_w[e])
            conv_s.append(jnp.stack([cache_conv[e][:, 1], u_s], axis=1))
            k_new, v_new = proj_s[:, k0:k0 + KV], proj_s[:, v0:v0 + KV]
            q_h = proj_s[:, q0:q0 + HQ].reshape(Bs, n_q, 1, HEAD_DIM)
            q_blk = (q_h * swa_diag.reshape(n_q, n_kv, HEAD_DIM)[None]).reshape(Bs, n_q, KV)
            sink_col = jnp.broadcast_to(sinks[e][:, None], (n_q, LANES))
            o_raw = _decode_t(q_blk, k_new, v_new, swa_bias, swa_kt, swa_vt, e, 8, sink=sink_col)
            o_raw = o_raw.reshape(Bs, n_kv, rep, n_kv, HEAD_DIM)
            yb_s = jnp.stack([o_raw[:, g, :, g, :] for g in range(n_kv)], axis=1).reshape(Bs, HQ)
            xs = _matmul_residual(xs, [ya_s, yb_s], w_out_e, e, Bs)
            swak_s.append(k_new.reshape(Bs, 1, n_kv, HEAD_DIM))
            swav_s.append(v_new.reshape(Bs, 1, n_kv, HEAD_DIM))
            cross_host = mem_host(sample_cross_q(xs, l), l)
        else:
            o = l // 2
            q0, k0, v0 = A, A + HQ, A + 2 * HQ
            yc, qkv, kt, vt, ptail = _inproj_odd(xp.reshape(B, L, D), norm_mix[l], w_in_o, o, pool_w_b[o],
                                                 pool_scale[o], TM)
            yd = _band_attn(qkv, 0, HQ // LANES, 2 * HQ // LANES, DIL_PATTERNS, False)
            mix_parts, w_mix, mix_layer = [yc, yd], w_out_o, o
            cross_host = None
            pool_p.append(ptail[:, POOL_CARRY - POOL_PAD:])
            dilk_p.append(kt)
            dilv_p.append(vt)

        xp = _cross_prompt(xp.reshape(B, L, D), mix_parts, w_mix, mix_layer, norm_ca[l], w_q, mkv_all, w_o, l,
                           TL, cross_host)
        if cross_host is not None:
            xp, o_s = xp
            xs = sample_ffn(xs, o_s, l)
        xp = xp.reshape(B * L, D)

        if l % 2 == 0 and l + 1 < depth:
            yc_s, host = sample_odd_front(xs, l + 1)
            xp, yd_s = _swiglu(xp, norm_ffn[l], w_f_in, w_f_out, l, TM, g_final, host)
            xs = _matmul_residual(xs, [yc_s, yd_s.reshape(Bs, HQ)], w_out_o, (l + 1) // 2, Bs)
        elif l % 2 == 1:
            xp, o_s = _swiglu(xp, norm_ffn[l], w_f_in, w_f_out, l, TM, g_final, mem_host(sample_cross_q(xs, l), l))
            xs = sample_ffn(xs, o_s, l)
        else:
            xp = _swiglu(xp, norm_ffn[l], w_f_in, w_f_out, l, TM, g_final)

    from_t = lambda ts: jnp.transpose(jnp.stack(ts).reshape(len(ts), B, n_q, HEAD_DIM, L), (0, 1, 4, 2, 3))
    return (xp.reshape(B, L, D), xs.reshape(Bs, 1, D),
            jnp.stack(conv_p), jnp.stack(swak_p), jnp.stack(swav_p),
            jnp.stack(pool_p), from_t(dilk_p), from_t(dilv_p),
            memk_p, memv_p,
            jnp.stack(conv_s), jnp.stack(swak_s), jnp.stack(swav_s),
            jnp.stack(pool_s), jnp.stack(dilk_s), jnp.stack(dilv_s))
```

```python
import functools

import jax
import jax.numpy as jnp
from jax import lax
from jax.experimental import pallas as pl
from jax.experimental.pallas import tpu as pltpu

F32 = jnp.float32
BF16 = jnp.bfloat16

LANES = 128
SUBLANES = 8
VMEM_LIMIT = 56 * 1024 * 1024

HEAD_DIM = 64
BLK = 128
CONV_W = 3
POOL_WINDOWS = (2, 4, 8, 16)
POOL_PAD = 15
POOL_GC = 128
DIL_PATTERNS = ((128, 1), (512, 4), (2048, 16))
CA_HEADS = 4
PAST_LEN = 8192
RMS_EPS = 1e-6
NEG_INF = -1e30


def _cparams(*sem):
    return pltpu.CompilerParams(dimension_semantics=sem, vmem_limit_bytes=VMEM_LIMIT)


def _rms(x, g):
    ms = jnp.mean(x * x, axis=-1, keepdims=True)
    return x * lax.rsqrt(ms + RMS_EPS) * g


def _dot(a, b):
    return jnp.dot(a, b, preferred_element_type=F32)


def _dot_nt(a, b):
    return lax.dot_general(a, b, (((1,), (1,)), ((), ())), preferred_element_type=F32)


def _resident(shape):
    return pl.BlockSpec(shape, lambda *_: (0,) * len(shape), pipeline_mode=pl.Buffered(1))


def _layer_block(stack, layer, rows=None, row_block=0):
    _, k, n = stack.shape
    return pl.BlockSpec((None, k if rows is None else rows, n), lambda *_: (layer, row_block, 0),
                        pipeline_mode=pl.Buffered(1))


def _norm_matmul_body(x_ref, g_ref, w_ref, o_ref):
    h = _rms(x_ref[...], g_ref[...]).astype(BF16)
    o_ref[...] = _dot(h, w_ref[...])


def _norm_matmul(x, g, w, layer, tm):
    m, d = x.shape
    n = w.shape[2]
    return pl.pallas_call(
        _norm_matmul_body,
        grid=(m // tm,),
        in_specs=[pl.BlockSpec((tm, d), lambda i: (i, 0)), _resident((1, d)), _layer_block(w, layer)],
        out_specs=pl.BlockSpec((tm, n), lambda i: (i, 0)),
        out_shape=jax.ShapeDtypeStruct((m, n), F32),
        compiler_params=_cparams("parallel"),
        name="norm_matmul",
    )(x, g.reshape(1, d), w)


def _matmul_residual_body(*refs, nparts):
    x_ref, o_ref = refs[0], refs[-1]
    acc = x_ref[...]
    for i in range(nparts):
        acc = acc + _dot(refs[1 + 2 * i][...].astype(BF16), refs[2 + 2 * i][...])
    o_ref[...] = acc


def _matmul_residual(x, parts, w, layer, tm):
    m, d = x.shape
    in_specs = [pl.BlockSpec((tm, d), lambda i: (i, 0))]
    args = [x]
    for row_block, a in enumerate(parts):
        in_specs += [pl.BlockSpec((tm, a.shape[1]), lambda i: (i, 0)), _layer_block(w, layer, a.shape[1], row_block)]
        args += [a, w]
    return pl.pallas_call(
        functools.partial(_matmul_residual_body, nparts=len(parts)),
        grid=(m // tm,),
        in_specs=in_specs,
        out_specs=pl.BlockSpec((tm, d), lambda i: (i, 0)),
        out_shape=jax.ShapeDtypeStruct((m, d), F32),
        compiler_params=_cparams("parallel"),
        name="matmul_residual",
    )(*args)


FFN_CHUNK = 256


def _swiglu_body(*refs, hidden, final, nsub, host):
    it = iter(refs)
    x_ref, g_ref, win_ref, wout_ref = next(it), next(it), next(it), next(it)
    gf_ref = next(it) if final else None
    if host is not None:
        host_in = [next(it) for _ in range(host[2])]
    o_ref = next(it)
    host_out = next(it) if host is not None else None
    h_scr, acc_scr = next(it), next(it)
    j = pl.program_id(1)

    @pl.when(j == 0)
    def _():
        x = x_ref[...]
        h_scr[...] = _rms(x, g_ref[...]).astype(BF16)
        acc_scr[...] = x

    nchunk = hidden // FFN_CHUNK
    per = -(-nchunk // nsub)
    for grp in range(nsub):
        @pl.when(j == grp)
        def _(grp=grp):
            h = h_scr[...]
            acc = acc_scr[...]
            first, last = grp * per, min((grp + 1) * per, nchunk)
            begin_at = first + 1 if last - first > 1 else first
            for c in range(first, last):
                if host is not None and c == begin_at:
                    state = host[0](*host_in, 0)
                lo = c * FFN_CHUNK
                gate = _dot(h, win_ref[:, lo:lo + FFN_CHUNK])
                up = _dot(h, win_ref[:, hidden + lo:hidden + lo + FFN_CHUNK])
                act = (gate * (1.0 / (1.0 + jnp.exp(-gate))) * up).astype(BF16)
                acc = acc + _dot(act, wout_ref[lo:lo + FFN_CHUNK, :])
            if host is not None:
                host[1](state, *host_in, host_out, 0)
            if grp < nsub - 1:
                acc_scr[...] = acc
            else:
                o_ref[...] = _rms(acc, gf_ref[...]) if final else acc


def _swiglu(x, g, win, wout, layer, tm, g_final=None, host=None):
    m, d = x.shape
    hidden = wout.shape[1]
    final = g_final is not None
    steps = m // tm
    in_specs = [pl.BlockSpec((tm, d), lambda i, j: (i, 0)), _resident((1, d)), _layer_block(win, layer),
                _layer_block(wout, layer)]
    args = [x, g.reshape(1, d), win, wout]
    if final:
        in_specs.append(_resident((1, d)))
        args.append(g_final.reshape(1, d))
    out_specs = [pl.BlockSpec((tm, d), lambda i, j: (i, 0))]
    out_shape = [jax.ShapeDtypeStruct((m, d), F32)]
    nsub, host_arg = 1, None
    if host is not None:
        begin_fn, end_fn, specs_fn, bs = host
        nsub = bs // steps
        h_specs, h_args, h_out_spec, h_out_shape = specs_fn(1, lambda i, j: i * nsub + j)
        in_specs += h_specs
        args += h_args
        out_specs.append(h_out_spec)
        out_shape.append(h_out_shape)
        host_arg = (begin_fn, end_fn, len(h_specs))
    res = pl.pallas_call(
        functools.partial(_swiglu_body, hidden=hidden, final=final, nsub=nsub, host=host_arg),
        grid=(steps, nsub),
        in_specs=in_specs,
        out_specs=out_specs,
        out_shape=out_shape,
        scratch_shapes=[pltpu.VMEM((tm, d), BF16), pltpu.VMEM((tm, d), F32)],
        compiler_params=_cparams("parallel", "arbitrary"),
        name="swiglu",
    )(*args)
    return res if host is not None else res[0]


def _inproj_even_body(x_ref, g_ref, w_ref, cw_ref, ya_ref, qkv_ref, tail_ref, ubuf, *, tm, width):
    @pl.when(pl.program_id(1) == 0)
    def _():
        ubuf[0:SUBLANES, :] = jnp.zeros((SUBLANES, width), F32)

    h = _rms(x_ref[0], g_ref[...]).astype(BF16)
    r = _dot(h, w_ref[...])
    qkv_ref[0] = r[:, 3 * width:]
    u = r[:, width:2 * width] * r[:, 2 * width:3 * width]
    ubuf[SUBLANES:SUBLANES + tm, :] = u
    conv = (cw_ref[0:1, :] * ubuf[SUBLANES - 2:SUBLANES - 2 + tm, :]
            + cw_ref[1:2, :] * ubuf[SUBLANES - 1:SUBLANES - 1 + tm, :]
            + cw_ref[2:3, :] * u)
    ya_ref[0] = (r[:, 0:width] * conv).astype(BF16)
    tail = ubuf[tm:tm + SUBLANES, :]
    tail_ref[0] = tail
    ubuf[0:SUBLANES, :] = tail


def _inproj_even(x, g, w, layer, cw, tm):
    b, l, d = x.shape
    n = w.shape[2]
    width = cw.shape[1]
    rest = n - 3 * width
    return pl.pallas_call(
        functools.partial(_inproj_even_body, tm=tm, width=width),
        grid=(b, l // tm),
        in_specs=[pl.BlockSpec((1, tm, d), lambda i, j: (i, j, 0)), _resident((1, d)), _layer_block(w, layer),
                  _resident(cw.shape)],
        out_specs=[pl.BlockSpec((1, tm, width), lambda i, j: (i, j, 0)),
                   pl.BlockSpec((1, tm, rest), lambda i, j: (i, j, 0)),
                   pl.BlockSpec((1, SUBLANES, width), lambda i, j: (i, 0, 0))],
        out_shape=[jax.ShapeDtypeStruct((b, l, width), BF16), jax.ShapeDtypeStruct((b, l, rest), F32),
                   jax.ShapeDtypeStruct((b, SUBLANES, width), F32)],
        scratch_shapes=[pltpu.VMEM((tm + SUBLANES, width), F32)],
        compiler_params=_cparams("parallel", "arbitrary"),
        name="inproj_even",
    )(x, g.reshape(1, d), w, cw)


POOL_CARRY = 16


def _inproj_odd_body(*refs, tm, width, n_prev):
    x_ref, g_ref, w_ref, pw_ref, sc_ref = refs[:5]
    prev_kt_ref, prev_vt_ref = refs[5:7] if n_prev else (None, None)
    yc_ref, qkv_ref, kt_ref, vt_ref, tail_ref, ubuf = refs[5 + 2 * bool(n_prev):]
    @pl.when(pl.program_id(1) == 0)
    def _():
        ubuf[0:POOL_CARRY, :] = jnp.zeros((POOL_CARRY, width), F32)

    h = _rms(x_ref[0], g_ref[...]).astype(BF16)
    r = _dot(h, w_ref[...])
    qkv_ref[0] = r[:, width:]
    if n_prev:
        kt_ref[0:n_prev, 0] = prev_kt_ref[:, 0]
        vt_ref[0:n_prev, 0] = prev_vt_ref[:, 0]
    kt_ref[n_prev, 0] = r[:, 2 * width:3 * width].T
    vt_ref[n_prev, 0] = r[:, 3 * width:4 * width].T
    u = r[:, 0:width]
    ubuf[POOL_CARRY:POOL_CARRY + tm, :] = u
    pos = pl.program_id(1) * tm + lax.broadcasted_iota(jnp.int32, (tm, 1), 0)
    outs = []
    for g, w in enumerate(POOL_WINDOWS):
        sl = slice(g * POOL_GC, (g + 1) * POOL_GC)
        ug = u[:, sl]
        acc = ug
        for i in range(1, w):
            acc = acc + ubuf[POOL_CARRY - i:POOL_CARRY - i + tm, sl]
        cnt = jnp.minimum(pos + 1, w).astype(F32)
        dlt = (acc / cnt - ug).astype(BF16)
        outs.append(_dot(dlt, pw_ref[g]))
    yc_ref[0] = (jnp.concatenate(outs, axis=1) * sc_ref[...]).astype(BF16)
    tail = ubuf[tm:tm + POOL_CARRY, :]
    tail_ref[0] = tail
    ubuf[0:POOL_CARRY, :] = tail


def _inproj_odd(x, g, w, layer, pw, scale, tm, prev_kt=None, prev_vt=None):
    b, l, d = x.shape
    n = w.shape[2]
    width = scale.shape[0]
    n_prev = 0 if prev_kt is None else prev_kt.shape[0]
    t_spec = pl.BlockSpec((n_prev + 1, 1, width, tm), lambda i, j: (0, i, 0, j))
    prev_specs = [pl.BlockSpec((n_prev, 1, width, tm), lambda i, j: (0, i, 0, j))] * 2 if n_prev else []
    prev_args = [prev_kt, prev_vt] if n_prev else []
    return pl.pallas_call(
        functools.partial(_inproj_odd_body, tm=tm, width=width, n_prev=n_prev),
        grid=(b, l // tm),
        in_specs=[pl.BlockSpec((1, tm, d), lambda i, j: (i, j, 0)), _resident((1, d)), _layer_block(w, layer),
                  _resident(pw.shape), _resident((1, width))] + prev_specs,
        out_specs=[pl.BlockSpec((1, tm, width), lambda i, j: (i, j, 0)),
                   pl.BlockSpec((1, tm, n - width), lambda i, j: (i, j, 0)), t_spec, t_spec,
                   pl.BlockSpec((1, POOL_CARRY, width), lambda i, j: (i, 0, 0))],
        out_shape=[jax.ShapeDtypeStruct((b, l, width), BF16), jax.ShapeDtypeStruct((b, l, n - width), F32),
                   jax.ShapeDtypeStruct((n_prev + 1, b, width, l), F32),
                   jax.ShapeDtypeStruct((n_prev + 1, b, width, l), F32),
                   jax.ShapeDtypeStruct((b, POOL_CARRY, width), F32)],
        scratch_shapes=[pltpu.VMEM((tm + POOL_CARRY, width), F32)],
        compiler_params=_cparams("parallel", "arbitrary"),
        name="inproj_odd",
    )(x, g.reshape(1, d), w, pw, scale.reshape(1, width), *prev_args)


TILES_IN_FLIGHT = 16


def _unroll(trips, limit=TILES_IN_FLIGHT):
    return max(u for u in range(1, max(limit, 1) + 1) if trips % u == 0)


def _band_attn_body(*refs, branches, gqa, sink, seq):
    nbr = len(branches)
    if sink:
        q_ref, k_ref, v_ref, sink_ref, y_ref = refs[:5]
        scr = refs[5:]
    else:
        q_ref, k_ref, v_ref, y_ref = refs[:4]
        scr = refs[4:]
    o_scr, l_scr = scr[:nbr], scr[nbr:2 * nbr]
    bias_first, bias_band = scr[2 * nbr], scr[2 * nbr + 1]
    lane = lax.broadcasted_iota(jnp.int32, (1, LANES), 1)
    lo = lane < HEAD_DIM

    rr = lax.broadcasted_iota(jnp.int32, (2 * BLK, 2 * BLK), 0) & (BLK - 1)
    cc = lax.broadcasted_iota(jnp.int32, (2 * BLK, 2 * BLK), 1)
    bias_band[...] = jnp.where((cc >= rr) & (cc <= rr + BLK), 0.0, NEG_INF).astype(F32)
    rr1 = lax.broadcasted_iota(jnp.int32, (2 * BLK, BLK), 0) & (BLK - 1)
    cc1 = lax.broadcasted_iota(jnp.int32, (2 * BLK, BLK), 1)
    bias_first[...] = jnp.where(cc1 <= rr1, 0.0, NEG_INF).astype(F32)

    if gqa:
        kx, vx = scr[2 * nbr + 2], scr[2 * nbr + 3]
        keep = (lane // HEAD_DIM) == (pl.program_id(1) // 2)
        chunk = 2 * BLK
        for c in range(seq // chunk):
            rows = slice(c * chunk, (c + 1) * chunk)
            kb = k_ref[0, rows, :]
            vb = v_ref[0, rows, :]
            kx[rows, :] = jnp.where(keep, kb, pltpu.roll(kb, HEAD_DIM, axis=1))
            vx[rows, :] = jnp.where(keep, vb, pltpu.roll(vb, HEAD_DIM, axis=1))
        load_k = lambda idx: kx[idx, :]
        load_v = lambda idx: vx[idx, :]
    else:
        load_k = lambda idx: k_ref[0, idx, :]
        load_v = lambda idx: v_ref[0, idx, :]
    load_q = lambda idx: q_ref[0, idx, :]

    scale = HEAD_DIM ** -0.5

    def tile(q_t, k_t, v_t, bias):
        qs = q_t * scale
        qm = jnp.concatenate([jnp.where(lo, qs, 0.0), jnp.where(lo, 0.0, qs)], axis=0).astype(BF16)
        s = _dot_nt(qm, k_t.astype(BF16)) + bias
        m = jnp.max(s, axis=1, keepdims=True)
        p = jnp.exp(s - m)
        den = jnp.sum(p, axis=1, keepdims=True)
        o2 = _dot(p.astype(BF16), v_t.astype(BF16)) / den
        lse = jnp.broadcast_to(m + jnp.log(den), (2 * BLK, LANES))
        o = jnp.where(lo, o2[:BLK], o2[BLK:])
        return o, jnp.where(lo, lse[:BLK], lse[BLK:])

    for bi, (w, d) in enumerate(branches):
        nblk = seq // (d * BLK)

        def rows(start, size, d=d):
            if isinstance(start, int):
                return pl.ds(start, size) if d == 1 else pl.ds(start, size, stride=d)
            if d == 1:
                return pl.ds(pl.multiple_of(start, BLK), size)
            return pl.ds(start, size, stride=d)

        def run_class(r, bi=bi, d=d, nblk=nblk, rows=rows):
            idx0 = rows(r, BLK)
            o, lse = tile(load_q(idx0), load_k(idx0), load_v(idx0), bias_first[...])
            o_scr[bi][idx0, :] = o
            l_scr[bi][idx0, :] = lse

            def body(i, carry):
                qi = rows(r + d * BLK * i, BLK)
                ki = rows(r + d * BLK * (i - 1), 2 * BLK)
                o, lse = tile(load_q(qi), load_k(ki), load_v(ki), bias_band[...])
                o_scr[bi][qi, :] = o
                l_scr[bi][qi, :] = lse
                return carry

            if nblk > 1:
                lax.fori_loop(1, nblk, body, 0, unroll=_unroll(nblk - 1))

        if d == 1:
            run_class(0)
        else:
            lax.fori_loop(0, d, lambda r, c, run_class=run_class: (run_class(r), c)[1], 0,
                          unroll=_unroll(d, TILES_IN_FLIGHT // min(nblk, TILES_IN_FLIGHT)))

    chunk = 2 * BLK
    for c in range(seq // chunk):
        rows_c = slice(c * chunk, (c + 1) * chunk)
        if sink:
            lse = l_scr[0][rows_c, :]
            gate = 1.0 / (1.0 + jnp.exp(sink_ref[...] - lse))
            y = o_scr[0][rows_c, :] * gate
        else:
            lses = [l_scr[bi][rows_c, :] for bi in range(nbr)]
            top = functools.reduce(jnp.maximum, lses)
            es = [jnp.exp(x - top) for x in lses]
            y = sum(e * o_scr[bi][rows_c, :] for bi, e in enumerate(es)) / sum(es)
        y_ref[0, rows_c, :] = y.astype(BF16)


def _band_attn(proj, q_blk, k_blk, v_blk, branches, gqa, sink_row=None):
    b, seq, _ = proj.shape
    nlb = 4
    nbr = len(branches)
    sink = sink_row is not None
    blk = (1, seq, LANES)
    kv_map = (lambda off: (lambda i, j: (i, 0, off))) if gqa else (lambda off: (lambda i, j: (i, 0, off + j)))
    in_specs = [pl.BlockSpec(blk, lambda i, j: (i, 0, q_blk + j)), pl.BlockSpec(blk, kv_map(k_blk)),
                pl.BlockSpec(blk, kv_map(v_blk))]
    args = [proj, proj, proj]
    if sink:
        in_specs.append(pl.BlockSpec((1, LANES), lambda i, j: (0, j)))
        args.append(sink_row)
    scratch = [pltpu.VMEM((seq, LANES), F32) for _ in range(2 * nbr)]
    scratch += [pltpu.VMEM((2 * BLK, BLK), F32), pltpu.VMEM((2 * BLK, 2 * BLK), F32)]
    if gqa:
        scratch += [pltpu.VMEM((seq, LANES), F32), pltpu.VMEM((seq, LANES), F32)]
    return pl.pallas_call(
        functools.partial(_band_attn_body, branches=branches, gqa=gqa, sink=sink, seq=seq),
        grid=(b, nlb),
        in_specs=in_specs,
        out_specs=pl.BlockSpec(blk, lambda i, j: (i, 0, j)),
        out_shape=jax.ShapeDtypeStruct((b, seq, nlb * LANES), BF16),
        scratch_shapes=scratch,
        compiler_params=_cparams("parallel", "parallel"),
        name="band_attn",
    )(*args)


def _cross_prompt_body(*refs, heads, nparts, host):
    x_ref = refs[0]
    g_ref, wq_ref, mkv_ref, wo_ref = refs[1 + 2 * nparts:5 + 2 * nparts]
    rest = refs[5 + 2 * nparts:]
    if host is not None:
        begin_fn, end_fn, n_in, bt = host
        host_in, o_ref, host_out = rest[:n_in], rest[n_in], rest[n_in + 1]
        states = [begin_fn(*host_in, b) for b in range(bt)]
    else:
        o_ref = rest[0]
    x = x_ref[0]
    for i in range(nparts):
        x = x + _dot(refs[1 + 2 * i][0], refs[2 + 2 * i][...])
    d = x.shape[1]
    dh = d // heads
    h = _rms(x, g_ref[...]).astype(BF16)
    q = (_dot(h, wq_ref[...]) * (dh ** -0.5)).astype(BF16)
    outs = []
    for c in range(heads):
        kc = mkv_ref[0, :, c * dh:(c + 1) * dh]
        vc = mkv_ref[0, :, d + c * dh:d + (c + 1) * dh]
        s = _dot_nt(q[:, c * dh:(c + 1) * dh], kc)
        m = jnp.max(s, axis=1, keepdims=True)
        p = jnp.exp(s - m)
        den = jnp.sum(p, axis=1, keepdims=True)
        outs.append((_dot(p.astype(BF16), vc) / den).astype(BF16))
    o_ref[0] = x + _dot(jnp.concatenate(outs, axis=1), wo_ref[...])
    if host is not None:
        for b in range(bt):
            end_fn(states[b], *host_in, host_out, b)


def _cross_prompt(x, parts, w_mix, mix_layer, g, wq, mkv_all, wo, layer, tl, host=None):
    b, l, d = x.shape
    mlen = mkv_all.shape[2]
    in_specs = [pl.BlockSpec((1, tl, d), lambda i, j: (i, j, 0))]
    args = [x]
    for row_block, a in enumerate(parts):
        in_specs += [pl.BlockSpec((1, tl, a.shape[2]), lambda i, j: (i, j, 0)),
                     _layer_block(w_mix, mix_layer, a.shape[2], row_block)]
        args += [a, w_mix]
    in_specs += [_resident((1, d)), _layer_block(wq, layer),
                 pl.BlockSpec((None, 1, mlen, 2 * d), lambda i, j: (layer, i, 0, 0)), _layer_block(wo, layer)]
    args += [g.reshape(1, d), wq, mkv_all, wo]
    out_specs = [pl.BlockSpec((1, tl, d), lambda i, j: (i, j, 0))]
    out_shape = [jax.ShapeDtypeStruct((b, l, d), F32)]
    host_arg = None
    if host is not None:
        begin_fn, end_fn, specs_fn, rows = host
        inner = l // tl
        bt = rows // (b * inner)
        h_specs, h_args, h_out_spec, h_out_shape = specs_fn(bt, lambda i, j: i * inner + j)
        in_specs += h_specs
        args += h_args
        out_specs.append(h_out_spec)
        out_shape.append(h_out_shape)
        host_arg = (begin_fn, end_fn, len(h_specs), bt)
    res = pl.pallas_call(
        functools.partial(_cross_prompt_body, heads=CA_HEADS, nparts=len(parts), host=host_arg),
        grid=(b, l // tl),
        in_specs=in_specs,
        out_specs=out_specs,
        out_shape=out_shape,
        compiler_params=_cparams("parallel", "parallel"),
        name="cross_prompt",
    )(*args)
    return res if host is not None else res[0]


def _memory_kv_body(x_ref, g_ref, w_ref, mkv_ref, kt_ref, vt_ref, *, heads, tm):
    h = _rms(x_ref[...], g_ref[...]).astype(BF16)
    r = _dot(h, w_ref[...])
    mkv_ref[...] = r.astype(BF16)
    d = r.shape[1] // 2
    halves = d // heads // LANES
    for t, out_ref in enumerate((kt_ref, vt_ref)):
        for hd in range(heads):
            for c in range(halves):
                col = t * d + hd * (d // heads) + c * LANES
                out_ref[pl.ds(c * heads + hd, tm, stride=heads * halves), :] = r[:, col:col + LANES]


def _memory_kv(mem, g_all, w_all, tm):
    m, d = mem.shape
    layers = w_all.shape[0]
    rows_per = d // LANES
    slab = pl.BlockSpec((None, tm * rows_per, LANES), lambda l, i: (l, i, 0))
    return pl.pallas_call(
        functools.partial(_memory_kv_body, heads=CA_HEADS, tm=tm),
        grid=(layers, m // tm),
        in_specs=[pl.BlockSpec((tm, d), lambda l, i: (i, 0)), pl.BlockSpec((None, 1, d), lambda l, i: (l, 0, 0)),
                  pl.BlockSpec((None, d, 2 * d), lambda l, i: (l, 0, 0))],
        out_specs=[pl.BlockSpec((None, tm, 2 * d), lambda l, i: (l, i, 0)), slab, slab],
        out_shape=[jax.ShapeDtypeStruct((layers, m, 2 * d), BF16),
                   jax.ShapeDtypeStruct((layers, m * rows_per, LANES), F32),
                   jax.ShapeDtypeStruct((layers, m * rows_per, LANES), F32)],
        compiler_params=_cparams("parallel", "parallel"),
        name="memory_kv",
    )(mem, g_all.reshape(layers, 1, d), w_all)


def _decode_t_scores(q_ref, knew_ref, bias_ref, sink_ref, k_ref, b, scale):
    qb = q_ref[b] * scale
    s = _dot(qb.astype(BF16), k_ref[0, b].astype(BF16))
    sn = jnp.sum(qb * knew_ref[b], axis=1, keepdims=True)
    stats = []
    for br in range(bias_ref.shape[0]):
        s_ = s + bias_ref[br]
        m = jnp.maximum(jnp.max(s_, axis=1, keepdims=True), sn)
        p = jnp.exp(s_ - m)
        pn = jnp.exp(sn - m)
        den = jnp.sum(p, axis=1, keepdims=True) + pn
        stats.append((p, pn, den, m + jnp.log(den)))
    if sink_ref is not None:
        facs = [1.0 / ((1.0 + jnp.exp(sink_ref[:, 0:1] - st[3])) * st[2]) for st in stats]
    else:
        top = functools.reduce(jnp.maximum, [st[3] for st in stats])
        es = [jnp.exp(st[3] - top) for st in stats]
        tot = sum(es)
        facs = [e / (tot * st[2]) for e, st in zip(es, stats)]
    p_all = sum(st[0] * f for st, f in zip(stats, facs))
    pn_all = sum(st[1] * f for st, f in zip(stats, facs))
    return p_all.astype(BF16), pn_all


def _decode_t_out(weights, vnew_ref, diag_ref, v_ref, o_ref, b):
    p_all, pn_all = weights
    o = _dot_nt(p_all, v_ref[0, b].astype(BF16)) + pn_all * vnew_ref[b]
    if diag_ref is not None:
        o_ref[b] = jnp.sum(o * diag_ref[...], axis=0, keepdims=True)
    else:
        o_ref[b] = o


def _decode_t_body(*refs, bt, sink, diag, scale):
    it = iter(refs)
    q_ref, knew_ref, vnew_ref, bias_ref = next(it), next(it), next(it), next(it)
    sink_ref = next(it) if sink else None
    diag_ref = next(it) if diag else None
    k_ref, v_ref, o_ref = next(it), next(it), next(it)
    weights = [_decode_t_scores(q_ref, knew_ref, bias_ref, sink_ref, k_ref, b, scale) for b in range(bt)]
    for b in range(bt):
        _decode_t_out(weights[b], vnew_ref, diag_ref, v_ref, o_ref, b)


def _decode_t_specs(q_blk, k_new, v_new, bias, kt_all, vt_all, layer, bt, sink, diag, row_of):
    bs, nh, W = q_blk.shape
    J = kt_all.shape[3]
    in_specs = [pl.BlockSpec((bt, nh, W), lambda *g: (row_of(*g), 0, 0)),
                pl.BlockSpec((bt, 1, W), lambda *g: (row_of(*g), 0, 0)),
                pl.BlockSpec((bt, 1, W), lambda *g: (row_of(*g), 0, 0)), _resident(bias.shape)]
    args = [q_blk, k_new.reshape(bs, 1, W), v_new.reshape(bs, 1, W), bias]
    for extra in (sink, diag):
        if extra is not None:
            in_specs.append(_resident(extra.shape))
            args.append(extra)
    cache_spec = pl.BlockSpec((1, bt, W, J), lambda *g: (layer, row_of(*g), 0, 0))
    in_specs += [cache_spec, cache_spec]
    args += [kt_all, vt_all]
    out_rows = 1 if diag is not None else nh
    out_spec = pl.BlockSpec((bt, out_rows, W), lambda *g: (row_of(*g), 0, 0))
    return in_specs, args, out_spec, jax.ShapeDtypeStruct((bs, out_rows, W), F32)


def _decode_t(q_blk, k_new, v_new, bias, kt_all, vt_all, layer, bt, sink=None, diag=None):
    in_specs, args, out_spec, out_shape = _decode_t_specs(q_blk, k_new, v_new, bias, kt_all, vt_all, layer, bt,
                                                          sink, diag, lambda i: i)
    return pl.pallas_call(
        functools.partial(_decode_t_body, bt=bt, sink=sink is not None, diag=diag is not None,
                          scale=HEAD_DIM ** -0.5),
        grid=(q_blk.shape[0] // bt,),
        in_specs=in_specs,
        out_specs=out_spec,
        out_shape=out_shape,
        compiler_params=_cparams("parallel"),
        name="decode_t",
    )(*args)


def _decode_mem_scores(q_ref, k_ref, b, scale):
    qb = q_ref[b] * scale
    rows = qb.shape[0]
    k3 = k_ref[0, b].reshape(-1, rows, LANES)
    part = jnp.sum(k3 * qb[None], axis=2, keepdims=True)
    s = part + pltpu.roll(part, rows // 2, axis=1)
    m = jnp.max(s, axis=0, keepdims=True)
    p = jnp.exp(s - m)
    return p / jnp.sum(p, axis=0, keepdims=True)


def _decode_mem_out(weights, v_ref, o_ref, b):
    rows = weights.shape[1]
    o_ref[b] = jnp.sum(weights * v_ref[0, b].reshape(-1, rows, LANES), axis=0)


def _decode_mem_specs(q, k_all, v_all, layer, bt, row_of):
    bs, rows, _ = q.shape
    cache_spec = pl.BlockSpec((1, bt, k_all.shape[2], LANES), lambda *g: (layer, row_of(*g), 0, 0))
    q_spec = pl.BlockSpec((bt, rows, LANES), lambda *g: (row_of(*g), 0, 0))
    return [q_spec, cache_spec, cache_spec], [q, k_all, v_all], q_spec, jax.ShapeDtypeStruct(q.shape, F32)


def _conv_sample_body(p_ref, b0_ref, b1_ref, cw_ref, ya_ref, u_ref, *, width):
    bg = p_ref[:, 0:width]
    u = p_ref[:, width:2 * width] * p_ref[:, 2 * width:3 * width]
    conv = cw_ref[0:1, :] * b0_ref[...] + cw_ref[1:2, :] * b1_ref[...] + cw_ref[2:3, :] * u
    ya_ref[...] = (bg * conv).astype(BF16)
    u_ref[...] = u


def _conv_sample(proj, buf0, buf1, cw):
    bs = proj.shape[0]
    width = cw.shape[1]
    full = lambda shape: pl.BlockSpec(shape, lambda i: (0,) * len(shape))
    return pl.pallas_call(
        functools.partial(_conv_sample_body, width=width),
        grid=(1,),
        in_specs=[full((bs, 3 * width)), full((bs, width)), full((bs, width)), full(cw.shape)],
        out_specs=[full((bs, width)), full((bs, width))],
        out_shape=[jax.ShapeDtypeStruct((bs, width), BF16), jax.ShapeDtypeStruct((bs, width), F32)],
        compiler_params=_cparams("arbitrary"),
        name="conv_sample",
    )(proj, buf0, buf1, cw)


def _pool_sample_body(p_ref, buf_ref, pw_ref, sc_ref, y_ref):
    u = p_ref[...]
    outs = []
    for g, w in enumerate(POOL_WINDOWS):
        sl = slice(g * POOL_GC, (g + 1) * POOL_GC)
        ug = u[:, sl]
        acc = ug
        for i in range(1, w):
            acc = acc + buf_ref[POOL_PAD - i][:, sl]
        cnt = float(min(PAST_LEN + 1, w))
        dlt = (acc / cnt - ug).astype(BF16)
        outs.append(_dot(dlt, pw_ref[g]))
    y_ref[...] = (jnp.concatenate(outs, axis=1) * sc_ref[...]).astype(BF16)


def _pool_sample(proj, buf_t, pw, scale):
    bs = proj.shape[0]
    width = scale.shape[0]
    full = lambda shape: pl.BlockSpec(shape, lambda i: (0,) * len(shape))
    return pl.pallas_call(
        _pool_sample_body,
        grid=(1,),
        in_specs=[full((bs, width)), full(buf_t.shape), full(pw.shape), full((1, width))],
        out_specs=full((bs, width)),
        out_shape=jax.ShapeDtypeStruct((bs, width), BF16),
        compiler_params=_cparams("arbitrary"),
        name="pool_sample",
    )(proj, buf_t, pw, scale.reshape(1, width))


def kernel(x_prompt, x_sample, mem_prompt, cache_conv, cache_swa_k, cache_swa_v, state_pool, cache_dil_k, cache_dil_v, cache_mem_k, cache_mem_v, norm_mix, norm_ca, norm_mem, norm_ffn, norm_final, w_in_even, conv_w, sinks, w_out_even, w_in_odd, pool_w, pool_scale, w_out_odd, w_ca_q, w_ca_kv, w_ca_o, w_ffn_in, w_ffn_out):
    B, L, D = x_prompt.shape
    Bs = x_sample.shape[0]
    depth = norm_mix.shape[0]
    mlen = mem_prompt.shape[1]
    A = conv_w.shape[2]
    HQ = D // 2
    n_q = HQ // HEAD_DIM
    n_kv = cache_swa_k.shape[3]
    rep = n_q // n_kv
    KV = n_kv * HEAD_DIM
    dh_ca = D // CA_HEADS
    TM = 512
    TL = 512

    w_in_e, w_out_e, w_in_o, w_out_o = (w.astype(BF16) for w in (w_in_even, w_out_even, w_in_odd, w_out_odd))
    w_q, w_kv, w_o, w_f_in, w_f_out = (w.astype(BF16) for w in (w_ca_q, w_ca_kv, w_ca_o, w_ffn_in, w_ffn_out))
    pool_w_b = pool_w.astype(BF16)

    xp = x_prompt.reshape(B * L, D)
    xs = x_sample.reshape(Bs, D)

    halves = dh_ca // LANES
    slab_rows = CA_HEADS * halves
    to_slab = lambda a: a.reshape(a.shape[:-2] + (CA_HEADS, halves, LANES)).swapaxes(-3, -2)
    from_slab = lambda a: a.swapaxes(-3, -2).reshape(a.shape[:-3] + (CA_HEADS, dh_ca))
    mkv_all, memk_t, memv_t = _memory_kv(mem_prompt.reshape(B * mlen, D), norm_mem, w_kv, TM)
    mkv_all = mkv_all.reshape(depth, B, mlen, 2 * D)
    memk_p = from_slab(memk_t.reshape(depth, B, mlen, halves, CA_HEADS, LANES))
    memv_p = from_slab(memv_t.reshape(depth, B, mlen, halves, CA_HEADS, LANES))
    mem_kc = to_slab(cache_mem_k).reshape(depth, Bs, mlen * slab_rows, LANES)
    mem_vc = to_slab(cache_mem_v).reshape(depth, Bs, mlen * slab_rows, LANES)

    swa_len, dil_len = cache_swa_k.shape[2], cache_dil_k.shape[2]
    to_t = lambda c: jnp.transpose(c, (0, 1, 3, 4, 2)).reshape(c.shape[0], Bs, c.shape[3] * HEAD_DIM, c.shape[2])
    swa_kt, swa_vt, dil_kt, dil_vt = to_t(cache_swa_k), to_t(cache_swa_v), to_t(cache_dil_k), to_t(cache_dil_v)

    pos = jnp.arange(dil_len)
    dil_bias = jnp.stack([jnp.where((pos >= dil_len - w) & ((dil_len - pos) % d == 0), 0.0, NEG_INF)
                          for w, d in DIL_PATTERNS]).astype(F32).reshape(len(DIL_PATTERNS), 1, dil_len)
    swa_bias = jnp.zeros((1, 1, swa_len), F32)
    head_of_row = jnp.arange(n_q)
    dil_diag = jnp.repeat(jnp.eye(n_q, dtype=F32), HEAD_DIM, axis=1)
    swa_diag = jnp.repeat(jax.nn.one_hot(head_of_row // rep, n_kv, dtype=F32), HEAD_DIM, axis=1)

    conv_p, swak_p, swav_p, pool_p = [], [], [], []
    dilk_t = dilv_t = None
    conv_s, swak_s, swav_s, pool_s, dilk_s, dilv_s = [], [], [], [], [], []
    attn_scale = HEAD_DIM ** -0.5

    def sample_cross_q(xs, l):
        q = _norm_matmul(xs, norm_ca[l], w_q, l, Bs).reshape(Bs, CA_HEADS, dh_ca)
        return to_slab(q).reshape(Bs, slab_rows, LANES)

    def mem_host(q_s, l):
        begin = lambda q, k, v, b: _decode_mem_scores(q, k, b, dh_ca ** -0.5)
        end = lambda weights, q, k, v, out, b: _decode_mem_out(weights, v, out, b)
        return begin, end, lambda bt, row_of: _decode_mem_specs(q_s, mem_kc, mem_vc, l, bt, row_of), Bs

    def sample_ffn(xs, o_s, l):
        o_s = from_slab(o_s.reshape(Bs, halves, CA_HEADS, LANES)).reshape(Bs, D)
        xs = _matmul_residual(xs, [o_s], w_o, l, Bs)
        return _swiglu(xs, norm_ffn[l], w_f_in, w_f_out, l, Bs, norm_final if l == depth - 1 else None)

    def sample_odd_front(xs, l):
        o = l // 2
        q0, k0, v0 = A, A + HQ, A + 2 * HQ
        proj_s = _norm_matmul(xs, norm_mix[l], w_in_o, o, Bs)
        yc_s = _pool_sample(proj_s, state_pool[o].transpose(1, 0, 2), pool_w_b[o], pool_scale[o])
        pool_s.append(jnp.concatenate([state_pool[o][:, 1:], proj_s[:, None, :A]], axis=1))
        k_new, v_new = proj_s[:, k0:k0 + HQ], proj_s[:, v0:v0 + HQ]
        dilk_s.append(k_new.reshape(Bs, 1, n_q, HEAD_DIM))
        dilv_s.append(v_new.reshape(Bs, 1, n_q, HEAD_DIM))
        q_blk = proj_s[:, None, q0:q0 + HQ] * dil_diag[None]
        begin = lambda q, kn, vn, bias, diag, k, v, b: _decode_t_scores(q, kn, bias, None, k, b, attn_scale)
        end = lambda weights, q, kn, vn, bias, diag, k, v, out, b: _decode_t_out(weights, vn, diag, v, out, b)
        specs_fn = lambda bt, row_of: _decode_t_specs(q_blk, k_new, v_new, dil_bias, dil_kt, dil_vt, o, bt, None,
                                                      dil_diag, row_of)
        return yc_s, (begin, end, specs_fn, Bs)

    for l in range(depth):
        g_final = norm_final if l == depth - 1 else None
        if l % 2 == 0:
            e = l // 2
            q0, k0, v0 = 3 * A, 3 * A + HQ, 3 * A + HQ + KV
            ya, qkv, tail = _inproj_even(xp.reshape(B, L, D), norm_mix[l], w_in_e, e, conv_w[e], TM)
            sink_row = jnp.repeat(sinks[e], HEAD_DIM).reshape(1, HQ)
            yb = _band_attn(qkv, 0, HQ // LANES, (HQ + KV) // LANES, ((BLK, 1),), True, sink_row)
            mix_parts, w_mix, mix_layer = [ya, yb], w_out_e, e
            conv_p.append(tail[:, SUBLANES - (CONV_W - 1):])
            swak_p.append(qkv[:, L - BLK:, HQ:HQ + KV].reshape(B, BLK, n_kv, HEAD_DIM))
            swav_p.append(qkv[:, L - BLK:, HQ + KV:].reshape(B, BLK, n_kv, HEAD_DIM))
            proj_s = _norm_matmul(xs, norm_mix[l], w_in_e, e, Bs)
            ya_s, u_s = _conv_sample(proj_s, cache_conv[e][:, 0], cache_conv[e][:, 1], conv_w[e])
            conv_s.append(jnp.stack([cache_conv[e][:, 1], u_s], axis=1))
            k_new, v_new = proj_s[:, k0:k0 + KV], proj_s[:, v0:v0 + KV]
            q_h = proj_s[:, q0:q0 + HQ].reshape(Bs, n_q, 1, HEAD_DIM)
            q_blk = (q_h * swa_diag.reshape(n_q, n_kv, HEAD_DIM)[None]).reshape(Bs, n_q, KV)
            sink_col = jnp.broadcast_to(sinks[e][:, None], (n_q, LANES))
            o_raw = _decode_t(q_blk, k_new, v_new, swa_bias, swa_kt, swa_vt, e, 8, sink=sink_col)
            o_raw = o_raw.reshape(Bs, n_kv, rep, n_kv, HEAD_DIM)
            yb_s = jnp.stack([o_raw[:, g, :, g, :] for g in range(n_kv)], axis=1).reshape(Bs, HQ)
            xs = _matmul_residual(xs, [ya_s, yb_s], w_out_e, e, Bs)
            swak_s.append(k_new.reshape(Bs, 1, n_kv, HEAD_DIM))
            swav_s.append(v_new.reshape(Bs, 1, n_kv, HEAD_DIM))
            cross_host = mem_host(sample_cross_q(xs, l), l)
        else:
            o = l // 2
            q0, k0, v0 = A, A + HQ, A + 2 * HQ
            yc, qkv, dilk_t, dilv_t, ptail = _inproj_odd(xp.reshape(B, L, D), norm_mix[l], w_in_o, o, pool_w_b[o],
                                                         pool_scale[o], TM, dilk_t, dilv_t)
            yd = _band_attn(qkv, 0, HQ // LANES, 2 * HQ // LANES, DIL_PATTERNS, False)
            mix_parts, w_mix, mix_layer = [yc, yd], w_out_o, o
            cross_host = None
            pool_p.append(ptail[:, POOL_CARRY - POOL_PAD:])

        xp = _cross_prompt(xp.reshape(B, L, D), mix_parts, w_mix, mix_layer, norm_ca[l], w_q, mkv_all, w_o, l,
                           TL, cross_host)
        if cross_host is not None:
            xp, o_s = xp
            xs = sample_ffn(xs, o_s, l)
        xp = xp.reshape(B * L, D)

        if l % 2 == 0 and l + 1 < depth:
            yc_s, host = sample_odd_front(xs, l + 1)
            xp, yd_s = _swiglu(xp, norm_ffn[l], w_f_in, w_f_out, l, TM, g_final, host)
            xs = _matmul_residual(xs, [yc_s, yd_s.reshape(Bs, HQ)], w_out_o, (l + 1) // 2, Bs)
        elif l % 2 == 1:
            xp, o_s = _swiglu(xp, norm_ffn[l], w_f_in, w_f_out, l, TM, g_final, mem_host(sample_cross_q(xs, l), l))
            xs = sample_ffn(xs, o_s, l)
        else:
            xp = _swiglu(xp, norm_ffn[l], w_f_in, w_f_out, l, TM, g_final)

    from_t = lambda t: jnp.transpose(t.reshape(t.shape[0], B, n_q, HEAD_DIM, L), (0, 1, 4, 2, 3))
    return (xp.reshape(B, L, D), xs.reshape(Bs, 1, D),
            jnp.stack(conv_p), jnp.stack(swak_p), jnp.stack(swav_p),
            jnp.stack(pool_p), from_t(dilk_t), from_t(dilv_t),
            memk_p, memv_p,
            jnp.stack(conv_s), jnp.stack(swak_s), jnp.stack(swav_s),
            jnp.stack(pool_s), jnp.stack(dilk_s), jnp.stack(dilv_s))
```

```python
import functools

import jax
import jax.numpy as jnp
from jax import lax
from jax.experimental import pallas as pl
from jax.experimental.pallas import tpu as pltpu

F32 = jnp.float32
BF16 = jnp.bfloat16

LANES = 128
SUBLANES = 8
VMEM_LIMIT = 56 * 1024 * 1024

HEAD_DIM = 64
BLK = 128
CONV_W = 3
POOL_WINDOWS = (2, 4, 8, 16)
POOL_PAD = 15
POOL_GC = 128
DIL_PATTERNS = ((128, 1), (512, 4), (2048, 16))
CA_HEADS = 4
PAST_LEN = 8192
RMS_EPS = 1e-6
NEG_INF = -1e30


def _cparams(*sem):
    return pltpu.CompilerParams(dimension_semantics=sem, vmem_limit_bytes=VMEM_LIMIT)


def _rms(x, g):
    ms = jnp.mean(x * x, axis=-1, keepdims=True)
    return x * lax.rsqrt(ms + RMS_EPS) * g


def _dot(a, b):
    return jnp.dot(a, b, preferred_element_type=F32)


def _dot_nt(a, b):
    return lax.dot_general(a, b, (((1,), (1,)), ((), ())), preferred_element_type=F32)


def _resident(shape):
    return pl.BlockSpec(shape, lambda *_: (0,) * len(shape), pipeline_mode=pl.Buffered(1))


def _layer_block(stack, layer, rows=None, row_block=0):
    _, k, n = stack.shape
    return pl.BlockSpec((None, k if rows is None else rows, n), lambda *_: (layer, row_block, 0),
                        pipeline_mode=pl.Buffered(1))


def _norm_matmul_body(x_ref, g_ref, w_ref, o_ref):
    h = _rms(x_ref[...], g_ref[...]).astype(BF16)
    o_ref[...] = _dot(h, w_ref[...])


def _norm_matmul(x, g, w, layer, tm):
    m, d = x.shape
    n = w.shape[2]
    return pl.pallas_call(
        _norm_matmul_body,
        grid=(m // tm,),
        in_specs=[pl.BlockSpec((tm, d), lambda i: (i, 0)), _resident((1, d)), _layer_block(w, layer)],
        out_specs=pl.BlockSpec((tm, n), lambda i: (i, 0)),
        out_shape=jax.ShapeDtypeStruct((m, n), F32),
        compiler_params=_cparams("parallel"),
        name="norm_matmul",
    )(x, g.reshape(1, d), w)


def _matmul_residual_body(*refs, nparts):
    x_ref, o_ref = refs[0], refs[-1]
    acc = x_ref[...]
    for i in range(nparts):
        acc = acc + _dot(refs[1 + 2 * i][...].astype(BF16), refs[2 + 2 * i][...])
    o_ref[...] = acc


def _matmul_residual(x, parts, w, layer, tm):
    m, d = x.shape
    in_specs = [pl.BlockSpec((tm, d), lambda i: (i, 0))]
    args = [x]
    for row_block, a in enumerate(parts):
        in_specs += [pl.BlockSpec((tm, a.shape[1]), lambda i: (i, 0)), _layer_block(w, layer, a.shape[1], row_block)]
        args += [a, w]
    return pl.pallas_call(
        functools.partial(_matmul_residual_body, nparts=len(parts)),
        grid=(m // tm,),
        in_specs=in_specs,
        out_specs=pl.BlockSpec((tm, d), lambda i: (i, 0)),
        out_shape=jax.ShapeDtypeStruct((m, d), F32),
        compiler_params=_cparams("parallel"),
        name="matmul_residual",
    )(*args)


FFN_CHUNK = 256


def _swiglu_body(*refs, hidden, final, nsub, host):
    it = iter(refs)
    x_ref, g_ref, win_ref, wout_ref = next(it), next(it), next(it), next(it)
    gf_ref = next(it) if final else None
    if host is not None:
        host_in = [next(it) for _ in range(host[2])]
    o_ref = next(it)
    host_out = next(it) if host is not None else None
    h_scr, acc_scr = next(it), next(it)
    j = pl.program_id(1)

    @pl.when(j == 0)
    def _():
        x = x_ref[...]
        h_scr[...] = _rms(x, g_ref[...]).astype(BF16)
        acc_scr[...] = x

    nchunk = hidden // FFN_CHUNK
    per = -(-nchunk // nsub)
    for grp in range(nsub):
        @pl.when(j == grp)
        def _(grp=grp):
            h = h_scr[...]
            acc = acc_scr[...]
            first, last = grp * per, min((grp + 1) * per, nchunk)
            begin_at = first + 1 if last - first > 1 else first
            for c in range(first, last):
                if host is not None and c == begin_at:
                    state = host[0](*host_in, 0)
                lo = c * FFN_CHUNK
                gate = _dot(h, win_ref[:, lo:lo + FFN_CHUNK])
                up = _dot(h, win_ref[:, hidden + lo:hidden + lo + FFN_CHUNK])
                act = (gate * (1.0 / (1.0 + jnp.exp(-gate))) * up).astype(BF16)
                acc = acc + _dot(act, wout_ref[lo:lo + FFN_CHUNK, :])
            if host is not None:
                host[1](state, *host_in, host_out, 0)
            if grp < nsub - 1:
                acc_scr[...] = acc
            else:
                o_ref[...] = _rms(acc, gf_ref[...]) if final else acc


def _swiglu(x, g, win, wout, layer, tm, g_final=None, host=None):
    m, d = x.shape
    hidden = wout.shape[1]
    final = g_final is not None
    steps = m // tm
    in_specs = [pl.BlockSpec((tm, d), lambda i, j: (i, 0)), _resident((1, d)), _layer_block(win, layer),
                _layer_block(wout, layer)]
    args = [x, g.reshape(1, d), win, wout]
    if final:
        in_specs.append(_resident((1, d)))
        args.append(g_final.reshape(1, d))
    out_specs = [pl.BlockSpec((tm, d), lambda i, j: (i, 0))]
    out_shape = [jax.ShapeDtypeStruct((m, d), F32)]
    nsub, host_arg = 1, None
    if host is not None:
        begin_fn, end_fn, specs_fn, bs = host
        nsub = bs // steps
        h_specs, h_args, h_out_spec, h_out_shape = specs_fn(1, lambda i, j: i * nsub + j)
        in_specs += h_specs
        args += h_args
        out_specs.append(h_out_spec)
        out_shape.append(h_out_shape)
        host_arg = (begin_fn, end_fn, len(h_specs))
    res = pl.pallas_call(
        functools.partial(_swiglu_body, hidden=hidden, final=final, nsub=nsub, host=host_arg),
        grid=(steps, nsub),
        in_specs=in_specs,
        out_specs=out_specs,
        out_shape=out_shape,
        scratch_shapes=[pltpu.VMEM((tm, d), BF16), pltpu.VMEM((tm, d), F32)],
        compiler_params=_cparams("parallel", "arbitrary"),
        name="swiglu",
    )(*args)
    return res if host is not None else res[0]


def _inproj_even_body(x_ref, g_ref, w_ref, cw_ref, ya_ref, qkv_ref, tail_ref, ubuf, *, tm, width):
    @pl.when(pl.program_id(1) == 0)
    def _():
        ubuf[0:SUBLANES, :] = jnp.zeros((SUBLANES, width), F32)

    h = _rms(x_ref[0], g_ref[...]).astype(BF16)
    r = _dot(h, w_ref[...])
    qkv_ref[0] = r[:, 3 * width:]
    u = r[:, width:2 * width] * r[:, 2 * width:3 * width]
    ubuf[SUBLANES:SUBLANES + tm, :] = u
    conv = (cw_ref[0:1, :] * ubuf[SUBLANES - 2:SUBLANES - 2 + tm, :]
            + cw_ref[1:2, :] * ubuf[SUBLANES - 1:SUBLANES - 1 + tm, :]
            + cw_ref[2:3, :] * u)
    ya_ref[0] = (r[:, 0:width] * conv).astype(BF16)
    tail = ubuf[tm:tm + SUBLANES, :]
    tail_ref[0] = tail
    ubuf[0:SUBLANES, :] = tail


def _inproj_even(x, g, w, layer, cw, tm):
    b, l, d = x.shape
    n = w.shape[2]
    width = cw.shape[1]
    rest = n - 3 * width
    return pl.pallas_call(
        functools.partial(_inproj_even_body, tm=tm, width=width),
        grid=(b, l // tm),
        in_specs=[pl.BlockSpec((1, tm, d), lambda i, j: (i, j, 0)), _resident((1, d)), _layer_block(w, layer),
                  _resident(cw.shape)],
        out_specs=[pl.BlockSpec((1, tm, width), lambda i, j: (i, j, 0)),
                   pl.BlockSpec((1, tm, rest), lambda i, j: (i, j, 0)),
                   pl.BlockSpec((1, SUBLANES, width), lambda i, j: (i, 0, 0))],
        out_shape=[jax.ShapeDtypeStruct((b, l, width), BF16), jax.ShapeDtypeStruct((b, l, rest), F32),
                   jax.ShapeDtypeStruct((b, SUBLANES, width), F32)],
        scratch_shapes=[pltpu.VMEM((tm + SUBLANES, width), F32)],
        compiler_params=_cparams("parallel", "arbitrary"),
        name="inproj_even",
    )(x, g.reshape(1, d), w, cw)


POOL_CARRY = 16


def _inproj_odd_body(*refs, tm, width, n_prev):
    x_ref, g_ref, w_ref, pw_ref, sc_ref = refs[:5]
    prev_kt_ref, prev_vt_ref = refs[5:7] if n_prev else (None, None)
    yc_ref, qkv_ref, kt_ref, vt_ref, tail_ref, ubuf = refs[5 + 2 * bool(n_prev):]
    @pl.when(pl.program_id(1) == 0)
    def _():
        ubuf[0:POOL_CARRY, :] = jnp.zeros((POOL_CARRY, width), F32)

    h = _rms(x_ref[0], g_ref[...]).astype(BF16)
    r = _dot(h, w_ref[...])
    qkv_ref[0] = r[:, width:]
    if n_prev:
        kt_ref[0:n_prev, 0] = prev_kt_ref[:, 0]
        vt_ref[0:n_prev, 0] = prev_vt_ref[:, 0]
    kt_ref[n_prev, 0] = r[:, 2 * width:3 * width].T
    vt_ref[n_prev, 0] = r[:, 3 * width:4 * width].T
    u = r[:, 0:width]
    ubuf[POOL_CARRY:POOL_CARRY + tm, :] = u
    pos = pl.program_id(1) * tm + lax.broadcasted_iota(jnp.int32, (tm, 1), 0)
    outs = []
    for g, w in enumerate(POOL_WINDOWS):
        sl = slice(g * POOL_GC, (g + 1) * POOL_GC)
        ug = u[:, sl]
        acc = ug
        for i in range(1, w):
            acc = acc + ubuf[POOL_CARRY - i:POOL_CARRY - i + tm, sl]
        cnt = jnp.minimum(pos + 1, w).astype(F32)
        dlt = (acc / cnt - ug).astype(BF16)
        outs.append(_dot(dlt, pw_ref[g]))
    yc_ref[0] = (jnp.concatenate(outs, axis=1) * sc_ref[...]).astype(BF16)
    tail = ubuf[tm:tm + POOL_CARRY, :]
    tail_ref[0] = tail
    ubuf[0:POOL_CARRY, :] = tail


def _inproj_odd(x, g, w, layer, pw, scale, tm, prev_kt=None, prev_vt=None):
    b, l, d = x.shape
    n = w.shape[2]
    width = scale.shape[0]
    n_prev = 0 if prev_kt is None else prev_kt.shape[0]
    t_spec = pl.BlockSpec((n_prev + 1, 1, width, tm), lambda i, j: (0, i, 0, j))
    prev_specs = [pl.BlockSpec((n_prev, 1, width, tm), lambda i, j: (0, i, 0, j))] * 2 if n_prev else []
    prev_args = [prev_kt, prev_vt] if n_prev else []
    return pl.pallas_call(
        functools.partial(_inproj_odd_body, tm=tm, width=width, n_prev=n_prev),
        grid=(b, l // tm),
        in_specs=[pl.BlockSpec((1, tm, d), lambda i, j: (i, j, 0)), _resident((1, d)), _layer_block(w, layer),
                  _resident(pw.shape), _resident((1, width))] + prev_specs,
        out_specs=[pl.BlockSpec((1, tm, width), lambda i, j: (i, j, 0)),
                   pl.BlockSpec((1, tm, n - width), lambda i, j: (i, j, 0)), t_spec, t_spec,
                   pl.BlockSpec((1, POOL_CARRY, width), lambda i, j: (i, 0, 0))],
        out_shape=[jax.ShapeDtypeStruct((b, l, width), BF16), jax.ShapeDtypeStruct((b, l, n - width), F32),
                   jax.ShapeDtypeStruct((n_prev + 1, b, width, l), F32),
                   jax.ShapeDtypeStruct((n_prev + 1, b, width, l), F32),
                   jax.ShapeDtypeStruct((b, POOL_CARRY, width), F32)],
        scratch_shapes=[pltpu.VMEM((tm + POOL_CARRY, width), F32)],
        compiler_params=_cparams("parallel", "arbitrary"),
        name="inproj_odd",
    )(x, g.reshape(1, d), w, pw, scale.reshape(1, width), *prev_args)


TILES_IN_FLIGHT = 16


def _unroll(trips, limit=TILES_IN_FLIGHT):
    return max(u for u in range(1, max(limit, 1) + 1) if trips % u == 0)


def _band_attn_body(*refs, branches, gqa, sink, seq):
    nbr = len(branches)
    if sink:
        q_ref, k_ref, v_ref, sink_ref, y_ref = refs[:5]
        scr = refs[5:]
    else:
        q_ref, k_ref, v_ref, y_ref = refs[:4]
        scr = refs[4:]
    o_scr, l_scr = scr[:nbr], scr[nbr:2 * nbr]
    bias_first, bias_band = scr[2 * nbr], scr[2 * nbr + 1]
    lane = lax.broadcasted_iota(jnp.int32, (1, LANES), 1)
    lo = lane < HEAD_DIM

    rr = lax.broadcasted_iota(jnp.int32, (2 * BLK, 2 * BLK), 0) & (BLK - 1)
    cc = lax.broadcasted_iota(jnp.int32, (2 * BLK, 2 * BLK), 1)
    bias_band[...] = jnp.where((cc >= rr) & (cc <= rr + BLK), 0.0, NEG_INF).astype(F32)
    rr1 = lax.broadcasted_iota(jnp.int32, (2 * BLK, BLK), 0) & (BLK - 1)
    cc1 = lax.broadcasted_iota(jnp.int32, (2 * BLK, BLK), 1)
    bias_first[...] = jnp.where(cc1 <= rr1, 0.0, NEG_INF).astype(F32)

    if gqa:
        kx, vx = scr[2 * nbr + 2], scr[2 * nbr + 3]
        keep = (lane // HEAD_DIM) == (pl.program_id(1) // 2)
        chunk = 2 * BLK
        for c in range(seq // chunk):
            rows = slice(c * chunk, (c + 1) * chunk)
            kb = k_ref[0, rows, :]
            vb = v_ref[0, rows, :]
            kx[rows, :] = jnp.where(keep, kb, pltpu.roll(kb, HEAD_DIM, axis=1))
            vx[rows, :] = jnp.where(keep, vb, pltpu.roll(vb, HEAD_DIM, axis=1))
        load_k = lambda idx: kx[idx, :]
        load_v = lambda idx: vx[idx, :]
    else:
        load_k = lambda idx: k_ref[0, idx, :]
        load_v = lambda idx: v_ref[0, idx, :]
    load_q = lambda idx: q_ref[0, idx, :]

    scale = HEAD_DIM ** -0.5

    def tile(q_t, k_t, v_t, bias):
        qs = q_t * scale
        qm = jnp.concatenate([jnp.where(lo, qs, 0.0), jnp.where(lo, 0.0, qs)], axis=0).astype(BF16)
        s = _dot_nt(qm, k_t.astype(BF16)) + bias
        m = jnp.max(s, axis=1, keepdims=True)
        p = jnp.exp(s - m)
        den = jnp.sum(p, axis=1, keepdims=True)
        o2 = _dot(p.astype(BF16), v_t.astype(BF16)) / den
        lse = jnp.broadcast_to(m + jnp.log(den), (2 * BLK, LANES))
        o = jnp.where(lo, o2[:BLK], o2[BLK:])
        return o, jnp.where(lo, lse[:BLK], lse[BLK:])

    for bi, (w, d) in enumerate(branches):
        nblk = seq // (d * BLK)

        def rows(start, size, d=d):
            if isinstance(start, int):
                return pl.ds(start, size) if d == 1 else pl.ds(start, size, stride=d)
            if d == 1:
                return pl.ds(pl.multiple_of(start, BLK), size)
            return pl.ds(start, size, stride=d)

        def run_class(r, bi=bi, d=d, nblk=nblk, rows=rows):
            idx0 = rows(r, BLK)
            o, lse = tile(load_q(idx0), load_k(idx0), load_v(idx0), bias_first[...])
            o_scr[bi][idx0, :] = o
            l_scr[bi][idx0, :] = lse

            def body(i, carry):
                qi = rows(r + d * BLK * i, BLK)
                ki = rows(r + d * BLK * (i - 1), 2 * BLK)
                o, lse = tile(load_q(qi), load_k(ki), load_v(ki), bias_band[...])
                o_scr[bi][qi, :] = o
                l_scr[bi][qi, :] = lse
                return carry

            if nblk > 1:
                lax.fori_loop(1, nblk, body, 0, unroll=_unroll(nblk - 1))

        if d == 1:
            run_class(0)
        else:
            lax.fori_loop(0, d, lambda r, c, run_class=run_class: (run_class(r), c)[1], 0,
                          unroll=_unroll(d, TILES_IN_FLIGHT // min(nblk, TILES_IN_FLIGHT)))

    chunk = 2 * BLK
    for c in range(seq // chunk):
        rows_c = slice(c * chunk, (c + 1) * chunk)
        if sink:
            lse = l_scr[0][rows_c, :]
            gate = 1.0 / (1.0 + jnp.exp(sink_ref[...] - lse))
            y = o_scr[0][rows_c, :] * gate
        else:
            lses = [l_scr[bi][rows_c, :] for bi in range(nbr)]
            top = functools.reduce(jnp.maximum, lses)
            es = [jnp.exp(x - top) for x in lses]
            y = sum(e * o_scr[bi][rows_c, :] for bi, e in enumerate(es)) / sum(es)
        y_ref[0, rows_c, :] = y.astype(BF16)


def _band_attn(proj, q_blk, k_blk, v_blk, branches, gqa, sink_row=None):
    b, seq, _ = proj.shape
    nlb = 4
    nbr = len(branches)
    sink = sink_row is not None
    blk = (1, seq, LANES)
    kv_map = (lambda off: (lambda i, j: (i, 0, off))) if gqa else (lambda off: (lambda i, j: (i, 0, off + j)))
    in_specs = [pl.BlockSpec(blk, lambda i, j: (i, 0, q_blk + j)), pl.BlockSpec(blk, kv_map(k_blk)),
                pl.BlockSpec(blk, kv_map(v_blk))]
    args = [proj, proj, proj]
    if sink:
        in_specs.append(pl.BlockSpec((1, LANES), lambda i, j: (0, j)))
        args.append(sink_row)
    scratch = [pltpu.VMEM((seq, LANES), F32) for _ in range(2 * nbr)]
    scratch += [pltpu.VMEM((2 * BLK, BLK), F32), pltpu.VMEM((2 * BLK, 2 * BLK), F32)]
    if gqa:
        scratch += [pltpu.VMEM((seq, LANES), F32), pltpu.VMEM((seq, LANES), F32)]
    return pl.pallas_call(
        functools.partial(_band_attn_body, branches=branches, gqa=gqa, sink=sink, seq=seq),
        grid=(b, nlb),
        in_specs=in_specs,
        out_specs=pl.BlockSpec(blk, lambda i, j: (i, 0, j)),
        out_shape=jax.ShapeDtypeStruct((b, seq, nlb * LANES), BF16),
        scratch_shapes=scratch,
        compiler_params=_cparams("parallel", "parallel"),
        name="band_attn",
    )(*args)


def _cross_prompt_body(*refs, heads, nparts, host):
    x_ref = refs[0]
    g_ref, wq_ref, mkv_ref, wo_ref = refs[1 + 2 * nparts:5 + 2 * nparts]
    rest = refs[5 + 2 * nparts:]
    if host is not None:
        begin_fn, end_fn, n_in, bt = host
        host_in, o_ref, host_out = rest[:n_in], rest[n_in], rest[n_in + 1]
        states = [begin_fn(*host_in, b) for b in range(bt)]
    else:
        o_ref = rest[0]
    x = x_ref[0]
    for i in range(nparts):
        x = x + _dot(refs[1 + 2 * i][0], refs[2 + 2 * i][...])
    d = x.shape[1]
    dh = d // heads
    h = _rms(x, g_ref[...]).astype(BF16)
    q = (_dot(h, wq_ref[...]) * (dh ** -0.5)).astype(BF16)
    outs = []
    for c in range(heads):
        kc = mkv_ref[0, :, c * dh:(c + 1) * dh]
        vc = mkv_ref[0, :, d + c * dh:d + (c + 1) * dh]
        s = _dot_nt(q[:, c * dh:(c + 1) * dh], kc)
        m = jnp.max(s, axis=1, keepdims=True)
        p = jnp.exp(s - m)
        den = jnp.sum(p, axis=1, keepdims=True)
        outs.append((_dot(p.astype(BF16), vc) / den).astype(BF16))
    o_ref[0] = x + _dot(jnp.concatenate(outs, axis=1), wo_ref[...])
    if host is not None:
        for b in range(bt):
            end_fn(states[b], *host_in, host_out, b)


def _cross_prompt(x, parts, w_mix, mix_layer, g, wq, mkv_all, wo, layer, tl, host=None):
    b, l, d = x.shape
    mlen = mkv_all.shape[2]
    in_specs = [pl.BlockSpec((1, tl, d), lambda i, j: (i, j, 0))]
    args = [x]
    for row_block, a in enumerate(parts):
        in_specs += [pl.BlockSpec((1, tl, a.shape[2]), lambda i, j: (i, j, 0)),
                     _layer_block(w_mix, mix_layer, a.shape[2], row_block)]
        args += [a, w_mix]
    in_specs += [_resident((1, d)), _layer_block(wq, layer),
                 pl.BlockSpec((None, 1, mlen, 2 * d), lambda i, j: (layer, i, 0, 0)), _layer_block(wo, layer)]
    args += [g.reshape(1, d), wq, mkv_all, wo]
    out_specs = [pl.BlockSpec((1, tl, d), lambda i, j: (i, j, 0))]
    out_shape = [jax.ShapeDtypeStruct((b, l, d), F32)]
    host_arg = None
    if host is not None:
        begin_fn, end_fn, specs_fn, rows = host
        inner = l // tl
        bt = rows // (b * inner)
        h_specs, h_args, h_out_spec, h_out_shape = specs_fn(bt, lambda i, j: i * inner + j)
        in_specs += h_specs
        args += h_args
        out_specs.append(h_out_spec)
        out_shape.append(h_out_shape)
        host_arg = (begin_fn, end_fn, len(h_specs), bt)
    res = pl.pallas_call(
        functools.partial(_cross_prompt_body, heads=CA_HEADS, nparts=len(parts), host=host_arg),
        grid=(b, l // tl),
        in_specs=in_specs,
        out_specs=out_specs,
        out_shape=out_shape,
        compiler_params=_cparams("parallel", "parallel"),
        name="cross_prompt",
    )(*args)
    return res if host is not None else res[0]


def _memory_kv_body(x_ref, g_ref, w_ref, mkv_ref, kt_ref, vt_ref, *, heads, tm):
    h = _rms(x_ref[...], g_ref[...]).astype(BF16)
    r = _dot(h, w_ref[...])
    mkv_ref[...] = r.astype(BF16)
    d = r.shape[1] // 2
    halves = d // heads // LANES
    for t, out_ref in enumerate((kt_ref, vt_ref)):
        for hd in range(heads):
            for c in range(halves):
                col = t * d + hd * (d // heads) + c * LANES
                out_ref[pl.ds(c * heads + hd, tm, stride=heads * halves), :] = r[:, col:col + LANES]


def _memory_kv(mem, g_all, w_all, tm):
    m, d = mem.shape
    layers = w_all.shape[0]
    rows_per = d // LANES
    slab = pl.BlockSpec((None, tm * rows_per, LANES), lambda l, i: (l, i, 0))
    return pl.pallas_call(
        functools.partial(_memory_kv_body, heads=CA_HEADS, tm=tm),
        grid=(layers, m // tm),
        in_specs=[pl.BlockSpec((tm, d), lambda l, i: (i, 0)), pl.BlockSpec((None, 1, d), lambda l, i: (l, 0, 0)),
                  pl.BlockSpec((None, d, 2 * d), lambda l, i: (l, 0, 0))],
        out_specs=[pl.BlockSpec((None, tm, 2 * d), lambda l, i: (l, i, 0)), slab, slab],
        out_shape=[jax.ShapeDtypeStruct((layers, m, 2 * d), BF16),
                   jax.ShapeDtypeStruct((layers, m * rows_per, LANES), F32),
                   jax.ShapeDtypeStruct((layers, m * rows_per, LANES), F32)],
        compiler_params=_cparams("parallel", "parallel"),
        name="memory_kv",
    )(mem, g_all.reshape(layers, 1, d), w_all)


def _decode_t_scores(q_ref, knew_ref, bias_ref, sink_ref, k_ref, b, scale):
    qb = q_ref[b] * scale
    s = _dot(qb.astype(BF16), k_ref[0, b].astype(BF16))
    sn = jnp.sum(qb * knew_ref[b], axis=1, keepdims=True)
    stats = []
    for br in range(bias_ref.shape[0]):
        s_ = s + bias_ref[br]
        m = jnp.maximum(jnp.max(s_, axis=1, keepdims=True), sn)
        p = jnp.exp(s_ - m)
        pn = jnp.exp(sn - m)
        den = jnp.sum(p, axis=1, keepdims=True) + pn
        stats.append((p, pn, den, m + jnp.log(den)))
    if sink_ref is not None:
        facs = [1.0 / ((1.0 + jnp.exp(sink_ref[:, 0:1] - st[3])) * st[2]) for st in stats]
    else:
        top = functools.reduce(jnp.maximum, [st[3] for st in stats])
        es = [jnp.exp(st[3] - top) for st in stats]
        tot = sum(es)
        facs = [e / (tot * st[2]) for e, st in zip(es, stats)]
    p_all = sum(st[0] * f for st, f in zip(stats, facs))
    pn_all = sum(st[1] * f for st, f in zip(stats, facs))
    return p_all.astype(BF16), pn_all


def _decode_t_out_vpu(weights, vnew_ref, diag_ref, v_ref, o_ref, b):
    p_all, pn_all = weights
    nh, W = diag_ref.shape
    v3 = v_ref[0, b].reshape(nh, W // nh, -1)
    col = jnp.sum(v3 * p_all.astype(F32)[:, None, :], axis=2, keepdims=True).reshape(W, 1)
    row = jnp.transpose(jnp.broadcast_to(col, (W, SUBLANES)))[0:1, :]
    o_ref[b] = row + jnp.sum(pn_all * vnew_ref[b] * diag_ref[...], axis=0, keepdims=True)


def _decode_t_out(weights, vnew_ref, diag_ref, v_ref, o_ref, b):
    p_all, pn_all = weights
    o = _dot_nt(p_all, v_ref[0, b].astype(BF16)) + pn_all * vnew_ref[b]
    if diag_ref is not None:
        o_ref[b] = jnp.sum(o * diag_ref[...], axis=0, keepdims=True)
    else:
        o_ref[b] = o


def _decode_t_body(*refs, bt, sink, diag, scale):
    it = iter(refs)
    q_ref, knew_ref, vnew_ref, bias_ref = next(it), next(it), next(it), next(it)
    sink_ref = next(it) if sink else None
    diag_ref = next(it) if diag else None
    k_ref, v_ref, o_ref = next(it), next(it), next(it)
    weights = [_decode_t_scores(q_ref, knew_ref, bias_ref, sink_ref, k_ref, b, scale) for b in range(bt)]
    for b in range(bt):
        _decode_t_out(weights[b], vnew_ref, diag_ref, v_ref, o_ref, b)


def _decode_t_specs(q_blk, k_new, v_new, bias, kt_all, vt_all, layer, bt, sink, diag, row_of):
    bs, nh, W = q_blk.shape
    J = kt_all.shape[3]
    in_specs = [pl.BlockSpec((bt, nh, W), lambda *g: (row_of(*g), 0, 0)),
                pl.BlockSpec((bt, 1, W), lambda *g: (row_of(*g), 0, 0)),
                pl.BlockSpec((bt, 1, W), lambda *g: (row_of(*g), 0, 0)), _resident(bias.shape)]
    args = [q_blk, k_new.reshape(bs, 1, W), v_new.reshape(bs, 1, W), bias]
    for extra in (sink, diag):
        if extra is not None:
            in_specs.append(_resident(extra.shape))
            args.append(extra)
    cache_spec = pl.BlockSpec((1, bt, W, J), lambda *g: (layer, row_of(*g), 0, 0))
    in_specs += [cache_spec, cache_spec]
    args += [kt_all, vt_all]
    out_rows = 1 if diag is not None else nh
    out_spec = pl.BlockSpec((bt, out_rows, W), lambda *g: (row_of(*g), 0, 0))
    return in_specs, args, out_spec, jax.ShapeDtypeStruct((bs, out_rows, W), F32)


def _decode_t(q_blk, k_new, v_new, bias, kt_all, vt_all, layer, bt, sink=None, diag=None):
    in_specs, args, out_spec, out_shape = _decode_t_specs(q_blk, k_new, v_new, bias, kt_all, vt_all, layer, bt,
                                                          sink, diag, lambda i: i)
    return pl.pallas_call(
        functools.partial(_decode_t_body, bt=bt, sink=sink is not None, diag=diag is not None,
                          scale=HEAD_DIM ** -0.5),
        grid=(q_blk.shape[0] // bt,),
        in_specs=in_specs,
        out_specs=out_spec,
        out_shape=out_shape,
        compiler_params=_cparams("parallel"),
        name="decode_t",
    )(*args)


def _decode_mem_scores(q_ref, k_ref, b, scale):
    qb = q_ref[b] * scale
    rows = qb.shape[0]
    k3 = k_ref[0, b].reshape(-1, rows, LANES)
    part = jnp.sum(k3 * qb[None], axis=2, keepdims=True)
    s = part + pltpu.roll(part, rows // 2, axis=1)
    m = jnp.max(s, axis=0, keepdims=True)
    p = jnp.exp(s - m)
    return p / jnp.sum(p, axis=0, keepdims=True)


def _decode_mem_out(weights, v_ref, o_ref, b):
    rows = weights.shape[1]
    o_ref[b] = jnp.sum(weights * v_ref[0, b].reshape(-1, rows, LANES), axis=0)


def _decode_mem_specs(q, k_all, v_all, layer, bt, row_of):
    bs, rows, _ = q.shape
    cache_spec = pl.BlockSpec((1, bt, k_all.shape[2], LANES), lambda *g: (layer, row_of(*g), 0, 0))
    q_spec = pl.BlockSpec((bt, rows, LANES), lambda *g: (row_of(*g), 0, 0))
    return [q_spec, cache_spec, cache_spec], [q, k_all, v_all], q_spec, jax.ShapeDtypeStruct(q.shape, F32)


def _conv_sample_body(p_ref, b0_ref, b1_ref, cw_ref, ya_ref, u_ref, *, width):
    bg = p_ref[:, 0:width]
    u = p_ref[:, width:2 * width] * p_ref[:, 2 * width:3 * width]
    conv = cw_ref[0:1, :] * b0_ref[...] + cw_ref[1:2, :] * b1_ref[...] + cw_ref[2:3, :] * u
    ya_ref[...] = (bg * conv).astype(BF16)
    u_ref[...] = u


def _conv_sample(proj, buf0, buf1, cw):
    bs = proj.shape[0]
    width = cw.shape[1]
    full = lambda shape: pl.BlockSpec(shape, lambda i: (0,) * len(shape))
    return pl.pallas_call(
        functools.partial(_conv_sample_body, width=width),
        grid=(1,),
        in_specs=[full((bs, 3 * width)), full((bs, width)), full((bs, width)), full(cw.shape)],
        out_specs=[full((bs, width)), full((bs, width))],
        out_shape=[jax.ShapeDtypeStruct((bs, width), BF16), jax.ShapeDtypeStruct((bs, width), F32)],
        compiler_params=_cparams("arbitrary"),
        name="conv_sample",
    )(proj, buf0, buf1, cw)


def _pool_sample_body(p_ref, buf_ref, pw_ref, sc_ref, y_ref):
    u = p_ref[...]
    outs = []
    for g, w in enumerate(POOL_WINDOWS):
        sl = slice(g * POOL_GC, (g + 1) * POOL_GC)
        ug = u[:, sl]
        acc = ug
        for i in range(1, w):
            acc = acc + buf_ref[POOL_PAD - i][:, sl]
        cnt = float(min(PAST_LEN + 1, w))
        dlt = (acc / cnt - ug).astype(BF16)
        outs.append(_dot(dlt, pw_ref[g]))
    y_ref[...] = (jnp.concatenate(outs, axis=1) * sc_ref[...]).astype(BF16)


def _pool_sample(proj, buf_t, pw, scale):
    bs = proj.shape[0]
    width = scale.shape[0]
    full = lambda shape: pl.BlockSpec(shape, lambda i: (0,) * len(shape))
    return pl.pallas_call(
        _pool_sample_body,
        grid=(1,),
        in_specs=[full((bs, width)), full(buf_t.shape), full(pw.shape), full((1, width))],
        out_specs=full((bs, width)),
        out_shape=jax.ShapeDtypeStruct((bs, width), BF16),
        compiler_params=_cparams("arbitrary"),
        name="pool_sample",
    )(proj, buf_t, pw, scale.reshape(1, width))


def kernel(x_prompt, x_sample, mem_prompt, cache_conv, cache_swa_k, cache_swa_v, state_pool, cache_dil_k, cache_dil_v, cache_mem_k, cache_mem_v, norm_mix, norm_ca, norm_mem, norm_ffn, norm_final, w_in_even, conv_w, sinks, w_out_even, w_in_odd, pool_w, pool_scale, w_out_odd, w_ca_q, w_ca_kv, w_ca_o, w_ffn_in, w_ffn_out):
    B, L, D = x_prompt.shape
    Bs = x_sample.shape[0]
    depth = norm_mix.shape[0]
    mlen = mem_prompt.shape[1]
    A = conv_w.shape[2]
    HQ = D // 2
    n_q = HQ // HEAD_DIM
    n_kv = cache_swa_k.shape[3]
    rep = n_q // n_kv
    KV = n_kv * HEAD_DIM
    dh_ca = D // CA_HEADS
    TM = 512
    TL = 512

    w_in_e, w_out_e, w_in_o, w_out_o = (w.astype(BF16) for w in (w_in_even, w_out_even, w_in_odd, w_out_odd))
    w_q, w_kv, w_o, w_f_in, w_f_out = (w.astype(BF16) for w in (w_ca_q, w_ca_kv, w_ca_o, w_ffn_in, w_ffn_out))
    pool_w_b = pool_w.astype(BF16)

    xp = x_prompt.reshape(B * L, D)
    xs = x_sample.reshape(Bs, D)

    halves = dh_ca // LANES
    slab_rows = CA_HEADS * halves
    to_slab = lambda a: a.reshape(a.shape[:-2] + (CA_HEADS, halves, LANES)).swapaxes(-3, -2)
    from_slab = lambda a: a.swapaxes(-3, -2).reshape(a.shape[:-3] + (CA_HEADS, dh_ca))
    mkv_all, memk_t, memv_t = _memory_kv(mem_prompt.reshape(B * mlen, D), norm_mem, w_kv, TM)
    mkv_all = mkv_all.reshape(depth, B, mlen, 2 * D)
    memk_p = from_slab(memk_t.reshape(depth, B, mlen, halves, CA_HEADS, LANES))
    memv_p = from_slab(memv_t.reshape(depth, B, mlen, halves, CA_HEADS, LANES))
    mem_kc = to_slab(cache_mem_k).reshape(depth, Bs, mlen * slab_rows, LANES)
    mem_vc = to_slab(cache_mem_v).reshape(depth, Bs, mlen * slab_rows, LANES)

    swa_len, dil_len = cache_swa_k.shape[2], cache_dil_k.shape[2]
    to_t = lambda c: jnp.transpose(c, (0, 1, 3, 4, 2)).reshape(c.shape[0], Bs, c.shape[3] * HEAD_DIM, c.shape[2])
    swa_kt, swa_vt, dil_kt, dil_vt = to_t(cache_swa_k), to_t(cache_swa_v), to_t(cache_dil_k), to_t(cache_dil_v)

    pos = jnp.arange(dil_len)
    dil_bias = jnp.stack([jnp.where((pos >= dil_len - w) & ((dil_len - pos) % d == 0), 0.0, NEG_INF)
                          for w, d in DIL_PATTERNS]).astype(F32).reshape(len(DIL_PATTERNS), 1, dil_len)
    swa_bias = jnp.zeros((1, 1, swa_len), F32)
    head_of_row = jnp.arange(n_q)
    dil_diag = jnp.repeat(jnp.eye(n_q, dtype=F32), HEAD_DIM, axis=1)
    swa_diag = jnp.repeat(jax.nn.one_hot(head_of_row // rep, n_kv, dtype=F32), HEAD_DIM, axis=1)

    conv_p, swak_p, swav_p, pool_p = [], [], [], []
    dilk_t = dilv_t = None
    conv_s, swak_s, swav_s, pool_s, dilk_s, dilv_s = [], [], [], [], [], []
    attn_scale = HEAD_DIM ** -0.5

    def sample_cross_q(xs, l):
        q = _norm_matmul(xs, norm_ca[l], w_q, l, Bs).reshape(Bs, CA_HEADS, dh_ca)
        return to_slab(q).reshape(Bs, slab_rows, LANES)

    def mem_host(q_s, l):
        begin = lambda q, k, v, b: _decode_mem_scores(q, k, b, dh_ca ** -0.5)
        end = lambda weights, q, k, v, out, b: _decode_mem_out(weights, v, out, b)
        return begin, end, lambda bt, row_of: _decode_mem_specs(q_s, mem_kc, mem_vc, l, bt, row_of), Bs

    def sample_ffn(xs, o_s, l):
        o_s = from_slab(o_s.reshape(Bs, halves, CA_HEADS, LANES)).reshape(Bs, D)
        xs = _matmul_residual(xs, [o_s], w_o, l, Bs)
        return _swiglu(xs, norm_ffn[l], w_f_in, w_f_out, l, Bs, norm_final if l == depth - 1 else None)

    def sample_odd_front(xs, l):
        o = l // 2
        q0, k0, v0 = A, A + HQ, A + 2 * HQ
        proj_s = _norm_matmul(xs, norm_mix[l], w_in_o, o, Bs)
        yc_s = _pool_sample(proj_s, state_pool[o].transpose(1, 0, 2), pool_w_b[o], pool_scale[o])
        pool_s.append(jnp.concatenate([state_pool[o][:, 1:], proj_s[:, None, :A]], axis=1))
        k_new, v_new = proj_s[:, k0:k0 + HQ], proj_s[:, v0:v0 + HQ]
        dilk_s.append(k_new.reshape(Bs, 1, n_q, HEAD_DIM))
        dilv_s.append(v_new.reshape(Bs, 1, n_q, HEAD_DIM))
        q_blk = proj_s[:, None, q0:q0 + HQ] * dil_diag[None]
        begin = lambda q, kn, vn, bias, diag, k, v, b: _decode_t_scores(q, kn, bias, None, k, b, attn_scale)
        end = lambda weights, q, kn, vn, bias, diag, k, v, out, b: _decode_t_out_vpu(weights, vn, diag, v, out, b)
        specs_fn = lambda bt, row_of: _decode_t_specs(q_blk, k_new, v_new, dil_bias, dil_kt, dil_vt, o, bt, None,
                                                      dil_diag, row_of)
        return yc_s, (begin, end, specs_fn, Bs)

    for l in range(depth):
        g_final = norm_final if l == depth - 1 else None
        if l % 2 == 0:
            e = l // 2
            q0, k0, v0 = 3 * A, 3 * A + HQ, 3 * A + HQ + KV
            ya, qkv, tail = _inproj_even(xp.reshape(B, L, D), norm_mix[l], w_in_e, e, conv_w[e], TM)
            sink_row = jnp.repeat(sinks[e], HEAD_DIM).reshape(1, HQ)
            yb = _band_attn(qkv, 0, HQ // LANES, (HQ + KV) // LANES, ((BLK, 1),), True, sink_row)
            mix_parts, w_mix, mix_layer = [ya, yb], w_out_e, e
            conv_p.append(tail[:, SUBLANES - (CONV_W - 1):])
            swak_p.append(qkv[:, L - BLK:, HQ:HQ + KV].reshape(B, BLK, n_kv, HEAD_DIM))
            swav_p.append(qkv[:, L - BLK:, HQ + KV:].reshape(B, BLK, n_kv, HEAD_DIM))
            proj_s = _norm_matmul(xs, norm_mix[l], w_in_e, e, Bs)
            ya_s, u_s = _conv_sample(proj_s, cache_conv[e][:, 0], cache_conv[e][:, 1], conv_w[e])
            conv_s.append(jnp.stack([cache_conv[e][:, 1], u_s], axis=1))
            k_new, v_new = proj_s[:, k0:k0 + KV], proj_s[:, v0:v0 + KV]
            q_h = proj_s[:, q0:q0 + HQ].reshape(Bs, n_q, 1, HEAD_DIM)
            q_blk = (q_h * swa_diag.reshape(n_q, n_kv, HEAD_DIM)[None]).reshape(Bs, n_q, KV)
            sink_col = jnp.broadcast_to(sinks[e][:, None], (n_q, LANES))
            o_raw = _decode_t(q_blk, k_new, v_new, swa_bias, swa_kt, swa_vt, e, 8, sink=sink_col)
            o_raw = o_raw.reshape(Bs, n_kv, rep, n_kv, HEAD_DIM)
            yb_s = jnp.stack([o_raw[:, g, :, g, :] for g in range(n_kv)], axis=1).reshape(Bs, HQ)
            xs = _matmul_residual(xs, [ya_s, yb_s], w_out_e, e, Bs)
            swak_s.append(k_new.reshape(Bs, 1, n_kv, HEAD_DIM))
            swav_s.append(v_new.reshape(Bs, 1, n_kv, HEAD_DIM))
            cross_host = mem_host(sample_cross_q(xs, l), l)
        else:
            o = l // 2
            q0, k0, v0 = A, A + HQ, A + 2 * HQ
            yc, qkv, dilk_t, dilv_t, ptail = _inproj_odd(xp.reshape(B, L, D), norm_mix[l], w_in_o, o, pool_w_b[o],
                                                         pool_scale[o], TM, dilk_t, dilv_t)
            yd = _band_attn(qkv, 0, HQ // LANES, 2 * HQ // LANES, DIL_PATTERNS, False)
            mix_parts, w_mix, mix_layer = [yc, yd], w_out_o, o
            cross_host = mem_host(sample_cross_q(xs, l), l)
            pool_p.append(ptail[:, POOL_CARRY - POOL_PAD:])

        xp = _cross_prompt(xp.reshape(B, L, D), mix_parts, w_mix, mix_layer, norm_ca[l], w_q, mkv_all, w_o, l,
                           TL, cross_host)
        if cross_host is not None:
            xp, o_s = xp
            xs = sample_ffn(xs, o_s, l)
        xp = xp.reshape(B * L, D)

        if l % 2 == 0 and l + 1 < depth:
            yc_s, host = sample_odd_front(xs, l + 1)
            xp, yd_s = _swiglu(xp, norm_ffn[l], w_f_in, w_f_out, l, TM, g_final, host)
            xs = _matmul_residual(xs, [yc_s, yd_s.reshape(Bs, HQ)], w_out_o, (l + 1) // 2, Bs)
        else:
            xp = _swiglu(xp, norm_ffn[l], w_f_in, w_f_out, l, TM, g_final)

    from_t = lambda t: jnp.transpose(t.reshape(t.shape[0], B, n_q, HEAD_DIM, L), (0, 1, 4, 2, 3))
    return (xp.reshape(B, L, D), xs.reshape(Bs, 1, D),
            jnp.stack(conv_p), jnp.stack(swak_p), jnp.stack(swav_p),
            jnp.stack(pool_p), from_t(dilk_t), from_t(dilv_t),
            memk_p, memv_p,
            jnp.stack(conv_s), jnp.stack(swak_s), jnp.stack(swav_s),
            jnp.stack(pool_s), jnp.stack(dilk_s), jnp.stack(dilv_s))
```
